```python
import functools
import jax, jax.numpy as jnp
from jax import lax
import numpy as np


D_MODEL = 4096
BATCH = 4
SEQ = 2048
DEPTH = 1
DEC_BATCH = 32
DEC_SEQ = 4
PAST_LEN = 8192
PAGE_SIZE = 128

HEAD_DIM = 128
HEADS_PER_GROUP = 8
DILATED_GROUPS = ((128, 1), (512, 4), (2048, 16))
N_GROUPS = len(DILATED_GROUPS)
ATTN_WIDTH = N_GROUPS * HEADS_PER_GROUP * HEAD_DIM
ATTN_OUT = HEADS_PER_GROUP * HEAD_DIM
ATTN_SCALE = HEAD_DIM ** -0.5
ROT_DIM = HEAD_DIM // 4
ROPE_THETA = 500000.0
CONV_CH = D_MODEL // 2
CONV_WIDTH = 31
N_EXPERTS = 32
TOP_K = 4
D_EXPERT = D_MODEL
SWIGLU_LIMIT = 7.0
SWIGLU_ALPHA = 1.702
MOE_BLOCK = 128
RMS_EPS = 1e-6
LN_EPS = 1e-5
IN_WIDTH = 3 * ATTN_WIDTH + 2 * CONV_CH + 2 * D_MODEL
IN_SPLITS = (ATTN_WIDTH, 2 * ATTN_WIDTH, 3 * ATTN_WIDTH, 3 * ATTN_WIDTH + 2 * CONV_CH)

kernel_name = 'hybrid_dilated_conformer_moe_step'


def rmsnorm(x, g):
    xf = x.astype(jnp.float32)
    y = xf * lax.rsqrt(jnp.mean(xf * xf, axis=-1, keepdims=True) + RMS_EPS)
    return (y * g.astype(jnp.float32)).astype(x.dtype)


def layernorm(x, g, b):
    xf = x.astype(jnp.float32)
    xc = xf - jnp.mean(xf, axis=-1, keepdims=True)
    y = xc * lax.rsqrt(jnp.mean(xc * xc, axis=-1, keepdims=True) + LN_EPS)
    return (y * g.astype(jnp.float32) + b.astype(jnp.float32)).astype(x.dtype)


def adaln_modulation(c, w_ada, b_ada):
    mod = jax.nn.silu(c) @ w_ada + b_ada
    return [m[:, None, :] for m in jnp.split(mod, 6, axis=-1)]


def modulate(x, g, shift, scale):
    return rmsnorm(x, g) * (1 + scale) + shift


def apply_rotary(x, pos):
    half = ROT_DIM // 2
    inv_freq = ROPE_THETA ** (-jnp.arange(half, dtype=jnp.float32) / half)
    ang = pos.astype(jnp.float32)[:, None] * inv_freq[None, :]
    cos = jnp.cos(ang)[:, None, :]
    sin = jnp.sin(ang)[:, None, :]
    xr = x[..., :ROT_DIM].astype(jnp.float32)
    x1, x2 = xr[..., :half], xr[..., half:]
    rot = jnp.concatenate([x1 * cos - x2 * sin, x2 * cos + x1 * sin], axis=-1).astype(x.dtype)
    return jnp.concatenate([rot, x[..., ROT_DIM:]], axis=-1)


def dilated_attn_prompt(q, k, v, window, dilation):
    bsz, seq, n_h, hd = q.shape
    span = window // dilation
    sub_len = seq // dilation
    n_blk = -(-sub_len // span)
    pad_len = n_blk * span
    n_cls = bsz * dilation

    def to_classes(t):
        t = t.reshape(bsz, sub_len, dilation, n_h, hd)
        t = jnp.moveaxis(t, 2, 1).reshape(n_cls, sub_len, n_h, hd)
        return jnp.pad(t, ((0, 0), (0, pad_len - sub_len), (0, 0), (0, 0)))

    def band(t):
        tp = jnp.pad(t, ((0, 0), (span, 0), (0, 0), (0, 0))).reshape(n_cls, n_blk + 1, span, n_h, hd)
        return jnp.concatenate([tp[:, :-1], tp[:, 1:]], axis=2)

    qb = to_classes(q).reshape(n_cls, n_blk, span, n_h, hd)
    kb = band(to_classes(k))
    vb = band(to_classes(v))
    s = jnp.einsum('nbqhd,nbkhd->nbhqk', qb, kb, preferred_element_type=jnp.float32) * ATTN_SCALE
    qi = jnp.arange(span)[:, None]
    kj = jnp.arange(2 * span)[None, :]
    dist = span + qi - kj
    blk = jnp.arange(n_blk)[:, None, None]
    valid = (dist >= 0) & (dist <= span) & (blk * span - span + kj >= 0)
    s = jnp.where(valid[None, :, None], s, -jnp.inf)
    m = jnp.max(s, axis=-1)
    p = jnp.exp(s - m[..., None])
    den = jnp.sum(p, axis=-1)
    num = jnp.einsum('nbhqk,nbkhd->nbqhd', p, vb.astype(jnp.float32))

    def from_classes(t):
        rest = t.shape[2:]
        t = t[:, :sub_len].reshape((bsz, dilation, sub_len) + rest)
        return jnp.moveaxis(t, 1, 2).reshape((bsz, seq) + rest)

    num = from_classes(num.reshape(n_cls, pad_len, n_h, hd))
    den = from_classes(jnp.swapaxes(den, 2, 3).reshape(n_cls, pad_len, n_h))
    m = from_classes(jnp.swapaxes(m, 2, 3).reshape(n_cls, pad_len, n_h))
    return num, den, m


def dilated_attn_sample(q, k_ext, v_ext, n_past, window, dilation):
    seq = q.shape[1]
    span = window // dilation
    idx = n_past + jnp.arange(seq)[:, None] - dilation * jnp.arange(span + 1)[None, :]
    valid = idx >= 0
    idx = jnp.maximum(idx, 0)
    kg = k_ext[:, idx]
    vg = v_ext[:, idx]
    s = jnp.einsum('bthd,btjhd->bthj', q, kg, preferred_element_type=jnp.float32) * ATTN_SCALE
    s = jnp.where(valid[None, :, None, :], s, -jnp.inf)
    m = jnp.max(s, axis=-1)
    p = jnp.exp(s - m[..., None])
    den = jnp.sum(p, axis=-1)
    num = jnp.einsum('bthj,btjhd->bthd', p, vg.astype(jnp.float32))
    return num, den, m


def combine_groups(parts):
    m_all = jnp.stack([pm for _, _, pm in parts])
    w = jnp.exp(m_all - jnp.max(m_all, axis=0))
    num = jnp.sum(w[..., None] * jnp.stack([pn for pn, _, _ in parts]), axis=0)
    den = jnp.sum(w * jnp.stack([pd for _, pd, _ in parts]), axis=0)
    return num / den[..., None]


def attend_prompt(q, k, v):
    parts, states = [], []
    for g, (win, dil) in enumerate(DILATED_GROUPS):
        parts.append(dilated_attn_prompt(q[:, :, g], k[:, :, g], v[:, :, g], win, dil))
        keep = min(win, q.shape[1])
        states.append(k[:, -keep:, g])
        states.append(v[:, -keep:, g])
    return combine_groups(parts), states


def attend_sample(q, k, v, k_bufs, v_bufs):
    parts, states = [], []
    for g, (win, dil) in enumerate(DILATED_GROUPS):
        n_past = k_bufs[g].shape[1]
        k_ext = jnp.concatenate([k_bufs[g].astype(k.dtype), k[:, :, g]], axis=1)
        v_ext = jnp.concatenate([v_bufs[g].astype(v.dtype), v[:, :, g]], axis=1)
        parts.append(dilated_attn_sample(q[:, :, g], k_ext, v_ext, n_past, win, dil))
        keep = min(win, n_past + q.shape[1])
        states.append(k_ext[:, -keep:])
        states.append(v_ext[:, -keep:])
    return combine_groups(parts), states


def token_mixer(h, pos, conv_prev, attend, w_in, w_proj_a, w_dw, b_dw, g_conv_ln, b_conv_ln, w_pw2, b_pw2, w_o):
    bsz, seq, _ = h.shape
    proj = h @ w_in
    q, k, v, glu, gates = jnp.split(proj, IN_SPLITS, axis=-1)
    n_heads = N_GROUPS * HEADS_PER_GROUP
    gshape = (bsz, seq, N_GROUPS, HEADS_PER_GROUP, HEAD_DIM)
    q = apply_rotary(q.reshape(bsz, seq, n_heads, HEAD_DIM), pos).reshape(gshape)
    k = apply_rotary(k.reshape(bsz, seq, n_heads, HEAD_DIM), pos).reshape(gshape)
    v = v.reshape(gshape)
    attn, kv_state = attend(q, k, v)
    branch_a = attn.reshape(bsz, seq, ATTN_OUT).astype(h.dtype) @ w_proj_a
    u_a, u_b = jnp.split(glu, 2, axis=-1)
    u = u_a * jax.nn.sigmoid(u_b)
    u_ext = jnp.concatenate([conv_prev.astype(u.dtype), u], axis=1)
    conv = lax.conv_general_dilated(u_ext, w_dw[:, None, :].astype(u.dtype), window_strides=(1,), padding='VALID',
                                    dimension_numbers=('NWC', 'WIO', 'NWC'), feature_group_count=CONV_CH) + b_dw
    branch_b = jax.nn.silu(layernorm(conv, g_conv_ln, b_conv_ln)) @ w_pw2 + b_pw2
    gate_a, gate_b = jnp.split(jax.nn.sigmoid(gates), 2, axis=-1)
    y = (gate_a * branch_a + gate_b * branch_b) @ w_o
    return y, kv_state, u_ext[:, -(CONV_WIDTH - 1):]


def moe_ffn(h, w_router, b_router, w_gate_up, b_gate_up, w_down, b_down):
    n_tok, d = h.shape
    logits = (h @ w_router + b_router).astype(jnp.float32)
    top_val, top_idx = lax.top_k(logits, TOP_K)
    gate = jax.nn.softmax(top_val, axis=-1)
    n_assign = n_tok * TOP_K
    flat_e = top_idx.reshape(n_assign)
    flat_tok = jnp.arange(n_assign, dtype=jnp.int32) // TOP_K
    flat_w = gate.reshape(n_assign)
    order = jnp.argsort(flat_e)
    e_sorted = flat_e[order]
    counts = jnp.bincount(flat_e, length=N_EXPERTS)
    padded = (counts + MOE_BLOCK - 1) // MOE_BLOCK * MOE_BLOCK
    starts = jnp.cumsum(counts) - counts
    pad_ends = jnp.cumsum(padded)
    pad_starts = pad_ends - padded
    dest = pad_starts[e_sorted] + jnp.arange(n_assign, dtype=jnp.int32) - starts[e_sorted]
    n_slots = -(-(n_assign + N_EXPERTS * (MOE_BLOCK - 1)) // MOE_BLOCK) * MOE_BLOCK
    n_blocks = n_slots // MOE_BLOCK
    slot_tok = jnp.zeros((n_slots,), jnp.int32).at[dest].set(flat_tok[order])
    slot_w = jnp.zeros((n_slots,), jnp.float32).at[dest].set(flat_w[order])
    block_e = jnp.minimum(jnp.searchsorted(pad_ends, jnp.arange(n_blocks, dtype=jnp.int32) * MOE_BLOCK, side='right'),
                          N_EXPERTS - 1)

    def expert_block(args):
        tok, e = args
        gu = h[tok] @ w_gate_up[e] + b_gate_up[e]
        g, u = jnp.split(gu, 2, axis=-1)
        g = jnp.minimum(g, SWIGLU_LIMIT)
        u = jnp.clip(u, -SWIGLU_LIMIT, SWIGLU_LIMIT)
        act = (u + 1) * (g * jax.nn.sigmoid(SWIGLU_ALPHA * g))
        return act @ w_down[e] + b_down[e]

    ys = lax.map(expert_block, (slot_tok.reshape(n_blocks, MOE_BLOCK), block_e))
    out = jnp.zeros((n_tok, d), jnp.float32).at[slot_tok].add(
        ys.reshape(n_slots, d).astype(jnp.float32) * slot_w[:, None])
    return out.astype(h.dtype)


def setup_inputs(seed: int = 0) -> dict:
    key = jax.random.key(seed)
    keys = iter(jax.random.split(key, 40))
    nrm = lambda shape, s: jax.random.normal(next(keys), shape, jnp.float32) * s
    gain = lambda shape: 1.0 + nrm(shape, 0.1)
    D, L = D_MODEL, DEPTH
    inp = {}
    inp['x_prompt'] = nrm((BATCH, SEQ, D), 1.0)
    inp['x_sample'] = nrm((DEC_BATCH, DEC_SEQ, D), 1.0)
    inp['c_prompt'] = nrm((BATCH, D), 1.0)
    inp['c_sample'] = nrm((DEC_BATCH, D), 1.0)
    for win, _ in DILATED_GROUPS:
        rows = min(win, PAST_LEN)
        inp['cache_k_w%d' % win] = nrm((L, DEC_BATCH, rows, HEADS_PER_GROUP, HEAD_DIM), 1.0)
        inp['cache_v_w%d' % win] = nrm((L, DEC_BATCH, rows, HEADS_PER_GROUP, HEAD_DIM), 1.0)
    inp['state_conv'] = nrm((L, DEC_BATCH, CONV_WIDTH - 1, CONV_CH), 0.5)
    inp['w_ada'] = nrm((L, D, 6 * D), 0.5 * D ** -0.5)
    inp['b_ada'] = nrm((L, 6 * D), 0.01)
    inp['g_pre_mix'] = gain((L, D))
    inp['g_post_mix'] = gain((L, D))
    inp['g_pre_ffn'] = gain((L, D))
    inp['g_post_ffn'] = gain((L, D))
    inp['w_in'] = nrm((L, D, IN_WIDTH), D ** -0.5)
    inp['w_proj_a'] = nrm((L, ATTN_OUT, D), ATTN_OUT ** -0.5)
    inp['w_dw'] = nrm((L, CONV_WIDTH, CONV_CH), CONV_WIDTH ** -0.5)
    inp['b_dw'] = nrm((L, CONV_CH), 0.01)
    inp['g_conv_ln'] = gain((L, CONV_CH))
    inp['b_conv_ln'] = nrm((L, CONV_CH), 0.01)
    inp['w_pw2'] = nrm((L, CONV_CH, D), CONV_CH ** -0.5)
    inp['b_pw2'] = nrm((L, D), 0.01)
    inp['w_o'] = nrm((L, D, D), D ** -0.5)
    inp['w_router'] = nrm((L, D, N_EXPERTS), D ** -0.5)
    inp['b_router'] = nrm((L, N_EXPERTS), 0.01)
    inp['w_gate_up'] = nrm((L, N_EXPERTS, D, 2 * D_EXPERT), D ** -0.5)
    inp['b_gate_up'] = nrm((L, N_EXPERTS, 2 * D_EXPERT), 0.01)
    inp['w_down'] = nrm((L, N_EXPERTS, D_EXPERT, D), D_EXPERT ** -0.5)
    inp['b_down'] = nrm((L, N_EXPERTS, D), 0.01)
    return inp


def reference(x_prompt, x_sample, c_prompt, c_sample, cache_k_w128, cache_v_w128, cache_k_w512, cache_v_w512,
              cache_k_w2048, cache_v_w2048, state_conv, w_ada, b_ada, g_pre_mix, g_post_mix, g_pre_ffn, g_post_ffn,
              w_in, w_proj_a, w_dw, b_dw, g_conv_ln, b_conv_ln, w_pw2, b_pw2, w_o, w_router, b_router,
              w_gate_up, b_gate_up, w_down, b_down):
    bsz, seq, d = x_prompt.shape
    dbsz, dseq, _ = x_sample.shape
    pos_prompt = jnp.arange(seq, dtype=jnp.int32)
    pos_sample = PAST_LEN + jnp.arange(dseq, dtype=jnp.int32)
    k_caches = (cache_k_w128, cache_k_w512, cache_k_w2048)
    v_caches = (cache_v_w128, cache_v_w512, cache_v_w2048)
    x_p, x_s = x_prompt, x_sample
    prompt_states, sample_states = [], []
    for l in range(DEPTH):
        sh_mp, sc_mp, gt_mp, sh_fp, sc_fp, gt_fp = adaln_modulation(c_prompt, w_ada[l], b_ada[l])
        sh_ms, sc_ms, gt_ms, sh_fs, sc_fs, gt_fs = adaln_modulation(c_sample, w_ada[l], b_ada[l])
        mixer_w = (w_in[l], w_proj_a[l], w_dw[l], b_dw[l], g_conv_ln[l], b_conv_ln[l], w_pw2[l], b_pw2[l], w_o[l])
        conv_zero = jnp.zeros((bsz, CONV_WIDTH - 1, CONV_CH), x_p.dtype)
        y_p, kv_p, conv_p = token_mixer(modulate(x_p, g_pre_mix[l], sh_mp, sc_mp), pos_prompt, conv_zero,
                                        attend_prompt, *mixer_w)
        attend_s = functools.partial(attend_sample, k_bufs=[c[l] for c in k_caches], v_bufs=[c[l] for c in v_caches])
        y_s, kv_s, conv_s = token_mixer(modulate(x_s, g_pre_mix[l], sh_ms, sc_ms), pos_sample, state_conv[l],
                                        attend_s, *mixer_w)
        x_p = x_p + gt_mp * rmsnorm(y_p, g_post_mix[l])
        x_s = x_s + gt_ms * rmsnorm(y_s, g_post_mix[l])
        h_p = modulate(x_p, g_pre_ffn[l], sh_fp, sc_fp).reshape(bsz * seq, d)
        h_s = modulate(x_s, g_pre_ffn[l], sh_fs, sc_fs).reshape(dbsz * dseq, d)
        f = moe_ffn(jnp.concatenate([h_p, h_s], axis=0), w_router[l], b_router[l], w_gate_up[l], b_gate_up[l],
                    w_down[l], b_down[l])
        x_p = x_p + gt_fp * rmsnorm(f[:bsz * seq].reshape(bsz, seq, d), g_post_ffn[l])
        x_s = x_s + gt_fs * rmsnorm(f[bsz * seq:].reshape(dbsz, dseq, d), g_post_ffn[l])
        prompt_states.append(kv_p + [conv_p])
        sample_states.append(kv_s + [conv_s])
    (new_k_w128_prompt, new_v_w128_prompt, new_k_w512_prompt, new_v_w512_prompt, new_k_w2048_prompt,
     new_v_w2048_prompt, new_conv_prompt) = [jnp.stack(s, axis=0) for s in zip(*prompt_states)]
    (new_k_w128_sample, new_v_w128_sample, new_k_w512_sample, new_v_w512_sample, new_k_w2048_sample,
     new_v_w2048_sample, new_conv_sample) = [jnp.stack(s, axis=0) for s in zip(*sample_states)]
    return (x_p, x_s, new_k_w128_prompt, new_v_w128_prompt, new_k_w512_prompt, new_v_w512_prompt,
            new_k_w2048_prompt, new_v_w2048_prompt, new_conv_prompt, new_k_w128_sample, new_v_w128_sample,
            new_k_w512_sample, new_v_w512_sample, new_k_w2048_sample, new_v_w2048_sample, new_conv_sample)
```

```python
import functools

import jax
import jax.numpy as jnp
from jax import lax
from jax.experimental import pallas as pl
from jax.experimental.pallas import tpu as pltpu

HEAD_DIM = 128
HEADS_PER_GROUP = 8
DILATED_GROUPS = ((128, 1), (512, 4), (2048, 16))
N_GROUPS = len(DILATED_GROUPS)
ATTN_WIDTH = N_GROUPS * HEADS_PER_GROUP * HEAD_DIM
ATTN_OUT = HEADS_PER_GROUP * HEAD_DIM
ATTN_SCALE = HEAD_DIM ** -0.5
ROT_DIM = HEAD_DIM // 4
ROPE_THETA = 500000.0
CONV_WIDTH = 31
PAST_LEN = 8192
TOP_K = 4
SWIGLU_LIMIT = 7.0
SWIGLU_ALPHA = 1.702
RMS_EPS = 1e-6
LN_EPS = 1e-5

LANES = 128
SUBLANES = 8
BF16_ROWS = 16
ATTN_BLOCK = 128
CONV_HALO = 32
VMEM_LIMIT = 56 * 1024 * 1024

_BF = jnp.bfloat16
_F32 = jnp.float32


def _pick(n, target, align):
    best = None
    for t in range(align, min(n, target) + 1, align):
        if n % t == 0:
            best = t
    return best if best is not None else n


def _params(sem):
    return pltpu.CompilerParams(dimension_semantics=sem, vmem_limit_bytes=VMEM_LIMIT)


def _rms(x, g):
    return x * lax.rsqrt(jnp.mean(x * x, axis=-1, keepdims=True) + RMS_EPS) * g


def _adaln_body(c_ref, w_ref, b_ref, o_ref):
    c = c_ref[...]
    a = (c * jax.nn.sigmoid(c)).astype(_BF)
    o_ref[...] = jnp.dot(a, w_ref[...].astype(_BF), preferred_element_type=_F32) + b_ref[...]


def _adaln(c_pad, w, b):
    rows, d = c_pad.shape
    n = w.shape[1]
    tn = _pick(n, 512, LANES)
    return pl.pallas_call(
        _adaln_body,
        grid=(n // tn,),
        in_specs=[pl.BlockSpec((rows, d), lambda j: (0, 0)),
                  pl.BlockSpec((d, tn), lambda j: (0, j)),
                  pl.BlockSpec((1, tn), lambda j: (0, j))],
        out_specs=pl.BlockSpec((rows, tn), lambda j: (0, j)),
        out_shape=jax.ShapeDtypeStruct((rows, n), _F32),
        compiler_params=_params(("parallel",)),
        name="adaln",
    )(c_pad, w, b.reshape(1, n))


def _tokenwise(body, rows_in, mods, gains, outs, mod, dims):
    B, S, DB, T, D = dims
    bs, dbt = B * S, DB * T
    tr = dbt
    n_p = bs // tr
    per_seq = S // tr
    last_p = lambda i: jnp.minimum(i, n_p - 1)

    in_specs, args, pick_p, pick_s = [], [], [], []

    def add(spec, arr):
        in_specs.append(spec)
        args.append(arr)
        return len(args) - 1

    for r in rows_in:
        if isinstance(r, tuple):
            pick_p.append(add(pl.BlockSpec((tr, r[0].shape[1]), lambda i: (last_p(i), 0)), r[0]))
            pick_s.append(add(pl.BlockSpec((tr, r[1].shape[1]), lambda i: (0, 0)), r[1]))
        else:
            k = add(pl.BlockSpec((tr, r.shape[1]), lambda i: (i, 0)), r)
            pick_p.append(k)
            pick_s.append(k)
    mod_p = mod[:B].reshape(B, 1, mod.shape[1])
    mod_s = jnp.repeat(mod[B:B + DB], T, axis=0)
    for w in mods:
        pick_p.append(add(pl.BlockSpec((None, 1, D), lambda i, w=w: (last_p(i) // per_seq, 0, w)), mod_p))
        pick_s.append(add(pl.BlockSpec((tr, D), lambda i, w=w: (0, w)), mod_s))
    for g in gains:
        k = add(pl.BlockSpec((1, D), lambda i: (0, 0)), g)
        pick_p.append(k)
        pick_s.append(k)
    n_in = len(args)
    out_specs, out_shape, out_p, out_s = [], [], [], []
    for dt, width, split in outs:
        if split:
            out_specs.append(pl.BlockSpec((tr, width), lambda i: (last_p(i), 0)))
            out_shape.append(jax.ShapeDtypeStruct((bs, width), dt))
            out_p.append(len(out_specs) - 1)
            out_specs.append(pl.BlockSpec((tr, width), lambda i: (0, 0)))
            out_shape.append(jax.ShapeDtypeStruct((dbt, width), dt))
            out_s.append(len(out_specs) - 1)
        else:
            out_specs.append(pl.BlockSpec((tr, width), lambda i: (i, 0)))
            out_shape.append(jax.ShapeDtypeStruct((bs + dbt, width), dt))
            out_p.append(len(out_specs) - 1)
            out_s.append(len(out_specs) - 1)

    def both(*refs):
        i = pl.program_id(0)

        @pl.when(i < n_p)
        def _():
            body(*[refs[k] for k in pick_p], *[refs[n_in + k] for k in out_p])

        @pl.when(i == n_p)
        def _():
            body(*[refs[k] for k in pick_s], *[refs[n_in + k] for k in out_s])

    res = pl.pallas_call(
        both, grid=(n_p + 1,), in_specs=in_specs, out_specs=out_specs, out_shape=out_shape,
        compiler_params=_params(("arbitrary",)), name=body.__name__.strip("_"),
    )(*args)
    return [(res[out_p[k]], res[out_s[k]]) if outs[k][2] else res[out_p[k]] for k in range(len(outs))]


def _pre_mix_body(x_ref, shift_ref, scale_ref, g_ref, h_ref):
    h = _rms(x_ref[...], g_ref[...]) * (1.0 + scale_ref[...]) + shift_ref[...]
    h_ref[...] = h.astype(_BF)


def _pack_bf16_pair(h):
    half = h.shape[1] // 2
    lo = pltpu.bitcast(h[:, :half].astype(_BF).astype(_F32), jnp.uint32) >> 16
    hi = pltpu.bitcast(h[:, half:].astype(_BF).astype(_F32), jnp.uint32) & jnp.uint32(0xFFFF0000)
    return hi | lo


def _post_mix_body(x_ref, y_ref, gate_ref, shift_ref, scale_ref, gpost_ref, gpre_ref, x1_ref, h2_ref, h2p_ref):
    x1 = x_ref[...] + gate_ref[...] * _rms(y_ref[...], gpost_ref[...])
    x1_ref[...] = x1
    h2 = _rms(x1, gpre_ref[...]) * (1.0 + scale_ref[...]) + shift_ref[...]
    h2_ref[...] = h2
    h2p_ref[...] = _pack_bf16_pair(h2)


def _post_ffn_body(x1_ref, f_ref, gate_ref, gpost_ref, o_ref):
    o_ref[...] = x1_ref[...] + gate_ref[...] * _rms(f_ref[...], gpost_ref[...])


def _mm_body(a_ref, w_ref, o_ref, acc_ref):
    k = pl.program_id(2)

    @pl.when(k == 0)
    def _():
        acc_ref[...] = jnp.zeros_like(acc_ref)

    acc_ref[...] += jnp.dot(a_ref[...], w_ref[...].astype(_BF), preferred_element_type=_F32)

    @pl.when(k == pl.num_programs(2) - 1)
    def _():
        o_ref[...] = acc_ref[...].astype(o_ref.dtype)


def _matmul(a, w, out_dtype, name):
    m, kd = a.shape
    n = w.shape[1]
    tm = _pick(m, 1664, BF16_ROWS)
    tn = _pick(n, 1024, LANES)
    tk = _pick(kd, 1024, LANES)
    return pl.pallas_call(
        _mm_body,
        grid=(m // tm, n // tn, kd // tk),
        in_specs=[pl.BlockSpec((tm, tk), lambda i, j, k: (i, k)),
                  pl.BlockSpec((tk, tn), lambda i, j, k: (k, j))],
        out_specs=pl.BlockSpec((tm, tn), lambda i, j, k: (i, j)),
        out_shape=jax.ShapeDtypeStruct((m, n), out_dtype),
        scratch_shapes=[pltpu.VMEM((tm, tn), _F32)],
        compiler_params=_params(("parallel", "parallel", "arbitrary")),
        name=name,
    )(a, w)


def _rope_tile(x, c, s_lo, s_hi):
    half = ROT_DIM // 2
    parts = []
    for hb in range(x.shape[1] // HEAD_DIM):
        xh = x[:, hb * HEAD_DIM:(hb + 1) * HEAD_DIM]
        up = pltpu.roll(xh, HEAD_DIM - half, 1)
        dn = pltpu.roll(xh, half, 1)
        parts.append(xh * c + up * s_lo + dn * s_hi)
    return jnp.concatenate(parts, axis=1) if len(parts) > 1 else parts[0]


def _in_proj_body(n_rope_tiles, a_ref, w_ref, c_ref, slo_ref, shi_ref, o_ref, acc_ref):
    j = pl.program_id(1)
    k = pl.program_id(2)
    last = pl.num_programs(2) - 1

    @pl.when(k == 0)
    def _():
        acc_ref[...] = jnp.zeros_like(acc_ref)

    acc_ref[...] += jnp.dot(a_ref[...], w_ref[...].astype(_BF), preferred_element_type=_F32)

    @pl.when((k == last) & (j < n_rope_tiles))
    def _():
        o_ref[...] = _rope_tile(acc_ref[...], c_ref[...], slo_ref[...], shi_ref[...])

    @pl.when((k == last) & (j >= n_rope_tiles))
    def _():
        o_ref[...] = acc_ref[...]


def _in_proj(h, w_in, tables):
    m, d = h.shape
    n = w_in.shape[1]
    tm = _pick(m, 1664, BF16_ROWS)
    tk = _pick(d, 1024, LANES)
    tn = LANES
    for t in range(LANES, 1024 + 1, LANES):
        if n % t == 0 and (2 * ATTN_WIDTH) % t == 0:
            tn = t
    tab_spec = pl.BlockSpec((tm, HEAD_DIM), lambda i, j, k: (i, 0))
    return pl.pallas_call(
        functools.partial(_in_proj_body, 2 * ATTN_WIDTH // tn),
        grid=(m // tm, n // tn, d // tk),
        in_specs=[pl.BlockSpec((tm, tk), lambda i, j, k: (i, k)),
                  pl.BlockSpec((tk, tn), lambda i, j, k: (k, j)),
                  tab_spec, tab_spec, tab_spec],
        out_specs=pl.BlockSpec((tm, tn), lambda i, j, k: (i, j)),
        out_shape=jax.ShapeDtypeStruct((m, n), _F32),
        scratch_shapes=[pltpu.VMEM((tm, tn), _F32)],
        compiler_params=_params(("parallel", "parallel", "arbitrary")),
        name="in_proj",
    )(h, w_in, *tables)


def _rope_tables(B, S, DB, T, past_len):
    half = ROT_DIM // 2
    pos = jnp.concatenate([jnp.tile(jnp.arange(S, dtype=jnp.int32), B),
                           jnp.tile(past_len + jnp.arange(T, dtype=jnp.int32), DB)])
    inv_freq = ROPE_THETA ** (-jnp.arange(half, dtype=_F32) / half)
    ang = pos.astype(_F32)[:, None] * inv_freq[None, :]
    cos, sin = jnp.cos(ang), jnp.sin(ang)
    rows = pos.shape[0]
    pad = HEAD_DIM - ROT_DIM
    c = jnp.concatenate([cos, cos, jnp.ones((rows, pad), _F32)], axis=1)
    s_lo = jnp.concatenate([-sin, jnp.zeros((rows, half + pad), _F32)], axis=1)
    s_hi = jnp.concatenate([jnp.zeros((rows, half), _F32), sin, jnp.zeros((rows, pad), _F32)], axis=1)
    return c, s_lo, s_hi


def _branch_merge_body(a1_ref, w1_ref, a2_ref, w2_ref, b2_ref, ga_ref, gb_ref, o_ref):
    ya = jnp.dot(a1_ref[...], w1_ref[...].astype(_BF), preferred_element_type=_F32)
    yb = jnp.dot(a2_ref[...], w2_ref[...].astype(_BF), preferred_element_type=_F32) + b2_ref[...]
    z = jax.nn.sigmoid(ga_ref[...]) * ya + jax.nn.sigmoid(gb_ref[...]) * yb
    o_ref[...] = z.astype(o_ref.dtype)


def _branch_merge(attn, w_proj_a, act, w_pw2, b_pw2, proj, gate_off):
    m, d = attn.shape[0], w_proj_a.shape[1]
    tm = _pick(m, 832, BF16_ROWS)
    tn = LANES
    for t in (256, 512):
        if d % t == 0 and gate_off % t == 0:
            tn = t
    ga_blk, gb_blk = gate_off // tn, (gate_off + d) // tn
    return pl.pallas_call(
        _branch_merge_body,
        grid=(m // tm, d // tn),
        in_specs=[pl.BlockSpec((tm, attn.shape[1]), lambda i, j: (i, 0)),
                  pl.BlockSpec((w_proj_a.shape[0], tn), lambda i, j: (0, j)),
                  pl.BlockSpec((tm, act.shape[1]), lambda i, j: (i, 0)),
                  pl.BlockSpec((w_pw2.shape[0], tn), lambda i, j: (0, j)),
                  pl.BlockSpec((1, tn), lambda i, j: (0, j)),
                  pl.BlockSpec((tm, tn), lambda i, j: (i, ga_blk + j)),
                  pl.BlockSpec((tm, tn), lambda i, j: (i, gb_blk + j))],
        out_specs=pl.BlockSpec((tm, tn), lambda i, j: (i, j)),
        out_shape=jax.ShapeDtypeStruct((m, d), _BF),
        compiler_params=_params(("parallel", "parallel")),
        name="branch_merge",
    )(attn, w_proj_a, act, w_pw2, b_pw2.reshape(1, d), proj, proj)


def _attn_prompt_body(*refs):
    q_refs, k_refs, v_refs, o_ref = refs[0:3], refs[3:6], refs[6:9], refs[9]
    blk = ATTN_BLOCK
    n_blk = o_ref.shape[0] // blk

    def q_block(i, _):
        r0 = pl.multiple_of(i * blk, blk)
        carry = (jnp.full((blk, 1), -jnp.inf, _F32), jnp.zeros((blk, 1), _F32), jnp.zeros((blk, HEAD_DIM), _F32))
        for g, (win, dil) in enumerate(DILATED_GROUPS):
            q = q_refs[g][pl.ds(r0, blk), :].astype(_BF)

            def k_block(j, c, g=g, win=win, dil=dil, q=q):
                m, l, acc = c
                c0 = pl.multiple_of(j * blk, blk)
                k = k_refs[g][pl.ds(c0, blk), :].astype(_BF)
                v = v_refs[g][pl.ds(c0, blk), :].astype(_BF)
                s = lax.dot_general(q, k, (((1,), (1,)), ((), ())), preferred_element_type=_F32) * ATTN_SCALE
                dist = (r0 - c0) + lax.broadcasted_iota(jnp.int32, (blk, blk), 0) - \
                    lax.broadcasted_iota(jnp.int32, (blk, blk), 1)
                valid = (dist >= 0) & (dist <= win) & (lax.rem(dist, dil) == 0)
                s = jnp.where(valid, s, -jnp.inf)
                m_new = jnp.maximum(m, jnp.max(s, axis=1, keepdims=True))
                m_safe = jnp.where(m_new == -jnp.inf, 0.0, m_new)
                alpha = jnp.exp(m - m_safe)
                p = jnp.exp(s - m_safe)
                l = alpha * l + jnp.sum(p, axis=1, keepdims=True)
                acc = alpha * acc + jnp.dot(p.astype(_BF), v, preferred_element_type=_F32)
                return m_new, l, acc

            carry = lax.fori_loop(jnp.maximum(i - win // blk, 0), i + 1, k_block, carry)
        _, l, acc = carry
        o_ref[pl.ds(r0, blk), :] = (acc / l).astype(o_ref.dtype)
        return 0

    lax.fori_loop(0, n_blk, q_block, 0)


def _attn_prompt(proj, B, S):
    H = HEADS_PER_GROUP
    specs = []
    for part in range(3):
        for g in range(N_GROUPS):
            off = part * N_GROUPS * H + g * H
            specs.append(pl.BlockSpec((S, HEAD_DIM), lambda b, h, off=off: (b, off + h)))
    return pl.pallas_call(
        _attn_prompt_body,
        grid=(B, H),
        in_specs=specs,
        out_specs=pl.BlockSpec((S, HEAD_DIM), lambda b, h: (b, h)),
        out_shape=jax.ShapeDtypeStruct((B * S, ATTN_OUT), _BF),
        compiler_params=_params(("parallel", "parallel")),
        name="attn_prompt",
    )(*([proj] * 9))


def _attn_sample_body(T, qkv_ref, *refs):
    kc_refs, vc_refs, o_ref = refs[0:3], refs[3:6], refs[6]
    H = HEADS_PER_GROUP
    qkv = qkv_ref[...]
    for t in range(T):
        scores, values = [], []
        for g, (win, dil) in enumerate(DILATED_GROUPS):
            q = qkv[t, g * H:(g + 1) * H, :]
            k_new = qkv[:, (N_GROUPS + g) * H:(N_GROUPS + g + 1) * H, :]
            v_new = qkv[:, (2 * N_GROUPS + g) * H:(2 * N_GROUPS + g + 1) * H, :]
            rows = win // dil
            if dil == 1:
                k_old, v_old = kc_refs[g][...], vc_refs[g][...]
                idx_old = lax.broadcasted_iota(jnp.int32, (rows, H, 1), 0)
            else:
                k_old, v_old = kc_refs[g][:, t], vc_refs[g][:, t]
                idx_old = t + dil * lax.broadcasted_iota(jnp.int32, (rows, H, 1), 0)
            idx_new = win + lax.broadcasted_iota(jnp.int32, (T, H, 1), 0)
            for kk, vv, idx in ((k_old, v_old, idx_old), (k_new, v_new, idx_new)):
                dist = (win + t) - idx
                valid = (dist >= 0) & (dist <= win) & (lax.rem(dist, dil) == 0)
                s = jnp.sum(kk * q[None], axis=-1, keepdims=True) * ATTN_SCALE
                scores.append(jnp.where(valid, s, -jnp.inf))
                values.append(vv)
        m = functools.reduce(jnp.maximum, [jnp.max(s, axis=0, keepdims=True) for s in scores])
        den = jnp.zeros((1, H, 1), _F32)
        num = jnp.zeros((1, H, HEAD_DIM), _F32)
        for s, vv in zip(scores, values):
            p = jnp.exp(s - m)
            den = den + jnp.sum(p, axis=0, keepdims=True)
            num = num + jnp.sum(p * vv, axis=0, keepdims=True)
        o_ref[pl.ds(t, 1)] = num / den


def _attn_sample(qkv_s, k_caches, v_caches, DB, T):
    H = HEADS_PER_GROUP
    args, specs = [qkv_s], [pl.BlockSpec((None, T, 9 * H, HEAD_DIM), lambda b: (b, 0, 0, 0))]
    for caches in (k_caches, v_caches):
        for g, (win, dil) in enumerate(DILATED_GROUPS):
            c = caches[g]
            assert c.shape[1] == win and win % dil == 0 and (dil == 1 or T <= dil)
            if dil == 1:
                args.append(c)
                specs.append(pl.BlockSpec((None, win, H, HEAD_DIM), lambda b: (b, 0, 0, 0)))
            else:
                args.append(c.reshape(DB, win // dil, dil, H, HEAD_DIM))
                specs.append(pl.BlockSpec((None, win // dil, T, H, HEAD_DIM), lambda b: (b, 0, 0, 0, 0)))
    return pl.pallas_call(
        functools.partial(_attn_sample_body, T),
        grid=(DB,),
        in_specs=specs,
        out_specs=pl.BlockSpec((None, T, H, HEAD_DIM), lambda b: (b, 0, 0, 0)),
        out_shape=jax.ShapeDtypeStruct((DB, T, H, HEAD_DIM), _F32),
        compiler_params=_params(("parallel",)),
        name="attn_sample",
    )(*args)


def _ln_silu(conv, g, b):
    xc = conv - jnp.mean(conv, axis=-1, keepdims=True)
    y = xc * lax.rsqrt(jnp.mean(xc * xc, axis=-1, keepdims=True) + LN_EPS) * g + b
    return y * jax.nn.sigmoid(y)


def _conv_prompt_body(n_piece, *refs):
    ua_refs, ub_refs = refs[:n_piece], refs[n_piece:2 * n_piece]
    wdw_ref, bdw_ref, g_ref, b_ref, act_ref, tail_ref, ext_ref, conv_ref = refs[2 * n_piece:]
    i = pl.program_id(1)
    tt = act_ref.shape[0]
    cw = ua_refs[0].shape[1]
    halo = CONV_HALO

    @pl.when(i == 0)
    def _():
        ext_ref[0:halo, :] = jnp.zeros((halo, ext_ref.shape[1]), _F32)

    for p in range(n_piece):
        ext_ref[halo:halo + tt, p * cw:(p + 1) * cw] = ua_refs[p][...] * jax.nn.sigmoid(ub_refs[p][...])

    rc = _pick(tt, 32, SUBLANES)
    cc = _pick(ext_ref.shape[1], 256, LANES)
    off = halo - (CONV_WIDTH - 1)
    win_rows = rc + halo

    def row_chunk(r, _):
        r0 = pl.multiple_of(r * rc, rc)
        for c0 in range(0, ext_ref.shape[1], cc):
            win = ext_ref[pl.ds(r0, win_rows), c0:c0 + cc]
            acc = jnp.broadcast_to(bdw_ref[:, c0:c0 + cc], (rc, cc))
            for res in range(SUBLANES):
                sh = win if res == 0 else pltpu.roll(win, win_rows - res, 0)
                for a0 in range(0, halo + 1, SUBLANES):
                    w = a0 + res - off
                    if 0 <= w < CONV_WIDTH:
                        acc = acc + sh[a0:a0 + rc] * wdw_ref[w:w + 1, c0:c0 + cc]
            conv_ref[pl.ds(r0, rc), c0:c0 + cc] = acc
        return 0

    lax.fori_loop(0, tt // rc, row_chunk, 0)
    act_ref[...] = _ln_silu(conv_ref[...], g_ref[...], b_ref[...]).astype(act_ref.dtype)
    tail = ext_ref[tt:tt + halo, :]
    tail_ref[...] = tail
    ext_ref[0:halo, :] = tail


def _conv_prompt(proj, w_dw, b_dw, g_ln, b_ln, B, S, glu_off):
    C = w_dw.shape[1]
    tt = _pick(S, 256, BF16_ROWS)
    nt = S // tt
    cw = LANES
    for t in range(LANES, 1024 + 1, LANES):
        if C % t == 0 and glu_off % t == 0:
            cw = t
    n_piece = C // cw
    specs = []
    for half in range(2):
        for p in range(n_piece):
            blk = (glu_off + half * C) // cw + p
            specs.append(pl.BlockSpec((tt, cw), lambda b, i, blk=blk: (b * nt + i, blk)))
    vec = lambda rows: pl.BlockSpec((rows, C), lambda b, i: (0, 0))
    act, tail = pl.pallas_call(
        functools.partial(_conv_prompt_body, n_piece),
        grid=(B, nt),
        in_specs=specs + [vec(CONV_WIDTH), vec(1), vec(1), vec(1)],
        out_specs=[pl.BlockSpec((tt, C), lambda b, i: (b * nt + i, 0)),
                   pl.BlockSpec((None, CONV_HALO, C), lambda b, i: (b, 0, 0))],
        out_shape=[jax.ShapeDtypeStruct((B * S, C), _BF), jax.ShapeDtypeStruct((B, CONV_HALO, C), _F32)],
        scratch_shapes=[pltpu.VMEM((CONV_HALO + tt, C), _F32), pltpu.VMEM((tt, C), _F32)],
        compiler_params=_params(("arbitrary", "arbitrary")),
        name="conv_prompt",
    )(*([proj] * (2 * n_piece)), w_dw, b_dw.reshape(1, C), g_ln.reshape(1, C), b_ln.reshape(1, C))
    return act, tail


def _conv_sample_body(T, glu_ref, state_ref, wst_ref, wu_ref, bdw_ref, g_ref, b_ref, act_ref, u_ref):
    C = u_ref.shape[2]
    glu = glu_ref[...]
    u = glu[:, :, :C] * jax.nn.sigmoid(glu[:, :, C:])
    u_ref[...] = u
    state = state_ref[...]
    for t in range(T):
        conv = jnp.sum(state * wst_ref[t][None], axis=1) + jnp.sum(u * wu_ref[t][None], axis=1) + bdw_ref[...]
        act_ref[:, t, :] = _ln_silu(conv, g_ref[...], b_ref[...])


def _conv_sample(glu_s, state, w_dw, b_dw, g_ln, b_ln):
    DB, T, _ = glu_s.shape
    C = w_dw.shape[1]
    n_st = CONV_WIDTH - 1
    r = jnp.arange(n_st)[None, :] - jnp.arange(T)[:, None]
    w_state = jnp.where((r >= 0)[..., None], w_dw[jnp.clip(r, 0, CONV_WIDTH - 1)], 0.0)
    ru = n_st - jnp.arange(T)[:, None] + jnp.arange(T)[None, :]
    w_new = jnp.where((ru <= n_st)[..., None], w_dw[jnp.clip(ru, 0, CONV_WIDTH - 1)], 0.0)
    full = lambda shape: pl.BlockSpec(shape, lambda i: (0,) * len(shape))
    act, u = pl.pallas_call(
        functools.partial(_conv_sample_body, T),
        grid=(1,),
        in_specs=[full(glu_s.shape), full(state.shape), full(w_state.shape), full(w_new.shape),
                  full((1, C)), full((1, C)), full((1, C))],
        out_specs=[full((DB, T, C)), full((DB, T, C))],
        out_shape=[jax.ShapeDtypeStruct((DB, T, C), _F32), jax.ShapeDtypeStruct((DB, T, C), _F32)],
        compiler_params=_params(("arbitrary",)),
        name="conv_sample",
    )(glu_s, state, w_state, w_new, b_dw.reshape(1, C), g_ln.reshape(1, C), b_ln.reshape(1, C))
    return act, u


def _router_body(n_exp, h_ref, w_ref, b_ref, idx_ref, gate_ref, rank_ref, cnt_ref, carry_ref):
    i = pl.program_id(0)
    tm = h_ref.shape[0]

    @pl.when(i == 0)
    def _():
        carry_ref[...] = jnp.zeros_like(carry_ref)

    logits = jnp.dot(h_ref[...], w_ref[...], preferred_element_type=_F32,
                     precision=lax.Precision.HIGHEST) + b_ref[...]
    lane = lax.broadcasted_iota(jnp.int32, (tm, n_exp), 1)
    lane_f = lane.astype(_F32)
    out_lane = lax.broadcasted_iota(jnp.int32, (tm, LANES), 1)
    work = logits
    vals, idxs = [], []
    for _ in range(TOP_K):
        v = jnp.max(work, axis=1, keepdims=True)
        ix = jnp.min(jnp.where(work == v, lane_f, float(n_exp)), axis=1, keepdims=True).astype(jnp.int32)
        vals.append(v)
        idxs.append(ix)
        work = jnp.where(lane == ix, -jnp.inf, work)
    exps = [jnp.exp(v - vals[0]) for v in vals]
    den = functools.reduce(jnp.add, exps)
    onehot = functools.reduce(jnp.add, [(lane == ix).astype(_F32) for ix in idxs])
    tri = (lax.broadcasted_iota(jnp.int32, (tm, tm), 1) < lax.broadcasted_iota(jnp.int32, (tm, tm), 0)).astype(_BF)
    before = jnp.dot(tri, onehot.astype(_BF), preferred_element_type=_F32) + carry_ref[...]
    idx_out = jnp.zeros((tm, LANES), jnp.int32)
    gate_out = jnp.zeros((tm, LANES), _F32)
    rank_out = jnp.zeros((tm, LANES), jnp.int32)
    for k in range(TOP_K):
        rk = jnp.sum(jnp.where(lane == idxs[k], before, 0.0), axis=1, keepdims=True).astype(jnp.int32)
        idx_out = jnp.where(out_lane == k, idxs[k], idx_out)
        gate_out = jnp.where(out_lane == k, exps[k] / den, gate_out)
        rank_out = jnp.where(out_lane == k, rk, rank_out)
    idx_ref[...] = idx_out
    gate_ref[...] = gate_out
    rank_ref[...] = rank_out
    carry_ref[...] += jnp.sum(onehot, axis=0, keepdims=True)
    cnt_ref[...] = carry_ref[...].astype(jnp.int32)


def _router(h2, w_router, b_router):
    m, d = h2.shape
    n_exp = w_router.shape[1]
    tm = _pick(m, 640, SUBLANES)
    row = lambda width: pl.BlockSpec((tm, width), lambda i: (i, 0))
    return pl.pallas_call(
        functools.partial(_router_body, n_exp),
        grid=(m // tm,),
        in_specs=[row(d), pl.BlockSpec((d, n_exp), lambda i: (0, 0)), pl.BlockSpec((1, n_exp), lambda i: (0, 0))],
        out_specs=[row(LANES), row(LANES), row(LANES), pl.BlockSpec((1, n_exp), lambda i: (0, 0))],
        out_shape=[jax.ShapeDtypeStruct((m, LANES), jnp.int32), jax.ShapeDtypeStruct((m, LANES), _F32),
                   jax.ShapeDtypeStruct((m, LANES), jnp.int32), jax.ShapeDtypeStruct((1, n_exp), jnp.int32)],
        scratch_shapes=[pltpu.VMEM((1, n_exp), _F32)],
        compiler_params=_params(("arbitrary",)),
        name="router",
    )(h2, w_router, b_router.reshape(1, n_exp))


def _moe_layout(m, n_exp):
    n_assign = m * TOP_K
    row_tile = 256 if n_assign >= 256 * n_exp else BF16_ROWS
    cap = -(-(n_assign * 3 // 2) // (n_exp * row_tile)) * row_tile
    n_chunks = n_exp + n_assign // cap
    return cap, row_tile, n_chunks


def _moe_plan(idx, rank, counts, cap, n_chunks):
    n_exp = counts.shape[0]
    chunks_per = (counts + cap - 1) // cap
    chunk_end = jnp.cumsum(chunks_per)
    chunk_start = chunk_end - chunks_per
    n_used = chunk_end[-1]
    dest = (chunk_start[idx] + rank // cap) * cap + rank % cap
    c = jnp.arange(n_chunks, dtype=jnp.int32)
    c_exp = jnp.minimum(jnp.searchsorted(chunk_end, c, side="right"), n_exp - 1).astype(jnp.int32)
    c_rows = jnp.clip(counts[c_exp] - (c - chunk_start[c_exp]) * cap, 0, cap)
    c_rows = jnp.where(c < n_used, c_rows, 0).astype(jnp.int32)
    return dest.astype(jnp.int32), c_exp, c_rows, n_used.astype(jnp.int32).reshape(1)


def _dispatch_body(cap, row_tile, c_rows_ref, dest_ref, h_ref, xs_ref, zero_ref, sem):
    i = pl.program_id(0)
    n_tok = dest_ref.shape[1] // TOP_K
    n_chunks = c_rows_ref.shape[0]

    @pl.when(i == 0)
    def _():
        zero_ref[...] = jnp.zeros_like(zero_ref)

        def clear(c, _):
            rows = c_rows_ref[c]

            @pl.when(rows % row_tile != 0)
            def _():
                start = pl.multiple_of(c * cap + (rows // row_tile) * row_tile, row_tile)
                cp = pltpu.make_async_copy(zero_ref, xs_ref.at[pl.ds(start, row_tile)], sem)
                cp.start()
                cp.wait()
            return 0

        lax.fori_loop(0, n_chunks, clear, 0)

    def row_copy(t, k):
        return pltpu.make_async_copy(h_ref.at[pl.ds(i * n_tok + t, 1)],
                                     xs_ref.at[pl.ds(dest_ref[0, t * TOP_K + k], 1)], sem)

    def issue(t, _):
        for k in range(TOP_K):
            row_copy(t, k).start()
        return 0

    def drain(t, _):
        for k in range(TOP_K):
            row_copy(t, k).wait()
        return 0

    lax.fori_loop(0, n_tok, issue, 0)
    lax.fori_loop(0, n_tok, drain, 0)


def _dispatch(h2p, dest, c_rows, cap, row_tile, n_chunks):
    m, half = h2p.shape
    td = _pick(m, 1024, SUBLANES)
    n_tiles = m // td
    return pl.pallas_call(
        functools.partial(_dispatch_body, cap, row_tile),
        grid_spec=pltpu.PrefetchScalarGridSpec(
            num_scalar_prefetch=1,
            grid=(n_tiles,),
            in_specs=[pl.BlockSpec((None, 1, td * TOP_K), lambda i, cr: (i, 0, 0), memory_space=pltpu.SMEM),
                      pl.BlockSpec(memory_space=pl.ANY)],
            out_specs=pl.BlockSpec(memory_space=pl.ANY),
            scratch_shapes=[pltpu.VMEM((row_tile, half), jnp.uint32), pltpu.SemaphoreType.DMA(())]),
        out_shape=jax.ShapeDtypeStruct((n_chunks * cap, half), jnp.uint32),
        compiler_params=_params(("arbitrary",)),
        name="moe_dispatch",
    )(c_rows, dest.reshape(n_tiles, 1, td * TOP_K), h2p)


def _unpack_bf16_pair(xp):
    lo = pltpu.bitcast(xp << 16, _F32).astype(_BF)
    hi = pltpu.bitcast(xp & jnp.uint32(0xFFFF0000), _F32).astype(_BF)
    return lo, hi


def _expert_up_body(row_tile, c_exp_ref, c_rows_ref, n_used_ref, x_ref, wg_ref, wu_ref, bg_ref, bu_ref, o_ref,
                    wgb_ref, wub_ref):
    c = pl.program_id(0)
    cap, half = x_ref.shape
    n_tiles = (c_rows_ref[c] + row_tile - 1) // row_tile

    @pl.when(c < n_used_ref[0])
    def _():
        wgb_ref[...] = wg_ref[...].astype(_BF)
        wub_ref[...] = wu_ref[...].astype(_BF)

        def tile(t, _):
            r0 = pl.multiple_of(t * row_tile, row_tile)
            lo, hi = _unpack_bf16_pair(x_ref[pl.ds(r0, row_tile), :])
            g = (jnp.dot(lo, wgb_ref[:half, :], preferred_element_type=_F32) +
                 jnp.dot(hi, wgb_ref[half:, :], preferred_element_type=_F32) + bg_ref[...])
            u = (jnp.dot(lo, wub_ref[:half, :], preferred_element_type=_F32) +
                 jnp.dot(hi, wub_ref[half:, :], preferred_element_type=_F32) + bu_ref[...])
            g = jnp.minimum(g, SWIGLU_LIMIT)
            u = jnp.clip(u, -SWIGLU_LIMIT, SWIGLU_LIMIT)
            act = (u + 1.0) * (g * jax.nn.sigmoid(SWIGLU_ALPHA * g))
            o_ref[pl.ds(r0, row_tile), :] = act.astype(o_ref.dtype)
            return 0

        lax.fori_loop(0, n_tiles, tile, 0)

        def clear(t, _):
            r0 = pl.multiple_of(t * row_tile, row_tile)
            o_ref[pl.ds(r0, row_tile), :] = jnp.zeros((row_tile, o_ref.shape[1]), o_ref.dtype)
            return 0

        lax.fori_loop(n_tiles, cap // row_tile, clear, 0)


def _chunk_maps(nj):
    def cc(c, nu):
        return jnp.minimum(c, nu[0] - 1)

    def jj(c, j, nu):
        return jnp.where(c < nu[0], j, nj - 1)

    return cc, jj


def _expert_up(xs, w_gate_up, b_gate_up, plan, cap, row_tile, n_chunks):
    c_exp, c_rows, n_used = plan
    n_exp, d, f2 = w_gate_up.shape
    f = f2 // 2
    tn = _pick(f, 256, LANES)
    nj = f // tn
    cc, jj = _chunk_maps(nj)
    b3 = b_gate_up.reshape(n_exp, 1, f2)
    return pl.pallas_call(
        functools.partial(_expert_up_body, row_tile),
        grid_spec=pltpu.PrefetchScalarGridSpec(
            num_scalar_prefetch=3,
            grid=(n_chunks, nj),
            in_specs=[
                pl.BlockSpec((cap, d // 2), lambda c, j, ce, cr, nu: (cc(c, nu), 0)),
                pl.BlockSpec((None, d, tn), lambda c, j, ce, cr, nu: (ce[cc(c, nu)], 0, jj(c, j, nu))),
                pl.BlockSpec((None, d, tn), lambda c, j, ce, cr, nu: (ce[cc(c, nu)], 0, nj + jj(c, j, nu))),
                pl.BlockSpec((None, 1, tn), lambda c, j, ce, cr, nu: (ce[cc(c, nu)], 0, jj(c, j, nu))),
                pl.BlockSpec((None, 1, tn), lambda c, j, ce, cr, nu: (ce[cc(c, nu)], 0, nj + jj(c, j, nu))),
            ],
            out_specs=pl.BlockSpec((cap, tn), lambda c, j, ce, cr, nu: (cc(c, nu), jj(c, j, nu))),
            scratch_shapes=[pltpu.VMEM((d, tn), _BF), pltpu.VMEM((d, tn), _BF)]),
        out_shape=jax.ShapeDtypeStruct((n_chunks * cap, f), _BF),
        compiler_params=_params(("arbitrary", "arbitrary")),
        name="expert_up",
    )(c_exp, c_rows, n_used, xs, w_gate_up, w_gate_up, b3, b3)


def _expert_down_body(row_tile, c_exp_ref, c_rows_ref, n_used_ref, a_ref, w_ref, b_ref, o_ref, wb_ref):
    c = pl.program_id(0)
    cap = a_ref.shape[0]
    n_tiles = (c_rows_ref[c] + row_tile - 1) // row_tile

    @pl.when(c < n_used_ref[0])
    def _():
        wb_ref[...] = w_ref[...].astype(_BF)

        def tile(t, _):
            r0 = pl.multiple_of(t * row_tile, row_tile)
            o_ref[pl.ds(r0, row_tile), :] = jnp.dot(a_ref[pl.ds(r0, row_tile), :], wb_ref[...],
                                                    preferred_element_type=_F32) + b_ref[...]
            return 0

        lax.fori_loop(0, n_tiles, tile, 0)

        def clear(t, _):
            r0 = pl.multiple_of(t * row_tile, row_tile)
            o_ref[pl.ds(r0, row_tile), :] = jnp.zeros((row_tile, o_ref.shape[1]), o_ref.dtype)
            return 0

        lax.fori_loop(n_tiles, cap // row_tile, clear, 0)


def _expert_down(act, w_down, b_down, plan, cap, row_tile, n_chunks):
    c_exp, c_rows, n_used = plan
    n_exp, f, d = w_down.shape
    tn = _pick(d, 512, LANES)
    nj = d // tn
    cc, jj = _chunk_maps(nj)
    return pl.pallas_call(
        functools.partial(_expert_down_body, row_tile),
        grid_spec=pltpu.PrefetchScalarGridSpec(
            num_scalar_prefetch=3,
            grid=(n_chunks, nj),
            in_specs=[
                pl.BlockSpec((cap, f), lambda c, j, ce, cr, nu: (cc(c, nu), 0)),
                pl.BlockSpec((None, f, tn), lambda c, j, ce, cr, nu: (ce[cc(c, nu)], 0, jj(c, j, nu))),
                pl.BlockSpec((None, 1, tn), lambda c, j, ce, cr, nu: (ce[cc(c, nu)], 0, jj(c, j, nu))),
            ],
            out_specs=pl.BlockSpec((cap, tn), lambda c, j, ce, cr, nu: (cc(c, nu), jj(c, j, nu))),
            scratch_shapes=[pltpu.VMEM((f, tn), _BF)]),
        out_shape=jax.ShapeDtypeStruct((n_chunks * cap, d), _F32),
        compiler_params=_params(("arbitrary", "arbitrary")),
        name="expert_down",
    )(c_exp, c_rows, n_used, act, w_down, b_down.reshape(n_exp, 1, d))


def _combine_body(dest_ref, gate_ref, y_ref, o_ref, buf_ref, sem):
    tc = o_ref.shape[0]

    def row_copy(t, k):
        return pltpu.make_async_copy(y_ref.at[pl.ds(dest_ref[0, t * TOP_K + k], 1)],
                                     buf_ref.at[k, pl.ds(t, 1)], sem)

    def issue(t, _):
        for k in range(TOP_K):
            row_copy(t, k).start()
        return 0

    def drain(t, _):
        for k in range(TOP_K):
            row_copy(t, k).wait()
        return 0

    lax.fori_loop(0, tc, issue, 0)
    lax.fori_loop(0, tc, drain, 0)
    gate = gate_ref[...]
    acc = buf_ref[0] * gate[:, 0:1]
    for k in range(1, TOP_K):
        acc = acc + buf_ref[k] * gate[:, k:k + 1]
    o_ref[...] = acc


def _combine(y, dest, gate):
    m = gate.shape[0]
    d = y.shape[1]
    tc = _pick(m, 128, SUBLANES)
    n_tiles = m // tc
    return pl.pallas_call(
        _combine_body,
        grid=(n_tiles,),
        in_specs=[pl.BlockSpec((None, 1, tc * TOP_K), lambda i: (i, 0, 0), memory_space=pltpu.SMEM),
                  pl.BlockSpec((tc, LANES), lambda i: (i, 0)),
                  pl.BlockSpec(memory_space=pl.ANY)],
        out_specs=pl.BlockSpec((tc, d), lambda i: (i, 0)),
        out_shape=jax.ShapeDtypeStruct((m, d), _F32),
        scratch_shapes=[pltpu.VMEM((TOP_K, tc, d), _F32), pltpu.SemaphoreType.DMA(())],
        compiler_params=_params(("arbitrary",)),
        name="moe_combine",
    )(dest.reshape(n_tiles, 1, tc * TOP_K), gate, y)


def _moe(h2, h2p, w_router, b_router, w_gate_up, b_gate_up, w_down, b_down):
    m = h2.shape[0]
    n_exp = w_router.shape[1]
    idx, gate, rank, counts = _router(h2, w_router, b_router)
    cap, row_tile, n_chunks = _moe_layout(m, n_exp)
    dest, c_exp, c_rows, n_used = _moe_plan(idx[:, :TOP_K], rank[:, :TOP_K], counts[0], cap, n_chunks)
    plan = (c_exp, c_rows, n_used)
    xs = _dispatch(h2p, dest, c_rows, cap, row_tile, n_chunks)
    act = _expert_up(xs, w_gate_up, b_gate_up, plan, cap, row_tile, n_chunks)
    y = _expert_down(act, w_down, b_down, plan, cap, row_tile, n_chunks)
    return _combine(y, dest, gate)


def kernel(x_prompt, x_sample, c_prompt, c_sample, cache_k_w128, cache_v_w128, cache_k_w512, cache_v_w512,
           cache_k_w2048, cache_v_w2048, state_conv, w_ada, b_ada, g_pre_mix, g_post_mix, g_pre_ffn, g_post_ffn,
           w_in, w_proj_a, w_dw, b_dw, g_conv_ln, b_conv_ln, w_pw2, b_pw2, w_o, w_router, b_router,
           w_gate_up, b_gate_up, w_down, b_down):
    B, S, D = x_prompt.shape
    DB, T, _ = x_sample.shape
    depth = w_ada.shape[0]
    C = w_dw.shape[2]
    H = HEADS_PER_GROUP
    bs, dbt = B * S, DB * T
    assert S % dbt == 0 and dbt % BF16_ROWS == 0 and S % ATTN_BLOCK == 0 and S >= CONV_HALO
    dims = (B, S, DB, T, D)
    k_caches = (cache_k_w128, cache_k_w512, cache_k_w2048)
    v_caches = (cache_v_w128, cache_v_w512, cache_v_w2048)
    glu_off = 3 * ATTN_WIDTH
    gate_off = glu_off + 2 * C
    tables = _rope_tables(B, S, DB, T, PAST_LEN)
    n_seq = B + DB
    c_all = jnp.concatenate([c_prompt, c_sample], axis=0)
    c_pad = jnp.pad(c_all, ((0, -n_seq % SUBLANES), (0, 0)))
    vec = lambda v: v.reshape(1, -1)

    x = (x_prompt.reshape(bs, D), x_sample.reshape(dbt, D))
    prompt_states, sample_states = [], []
    for l in range(depth):
        mod = _adaln(c_pad, w_ada[l], b_ada[l])
        (h,) = _tokenwise(_pre_mix_body, [x], [0, 1], [vec(g_pre_mix[l])], [(_BF, D, False)], mod, dims)
        proj = _in_proj(h, w_in[l], tables)

        attn_p = _attn_prompt(proj, B, S)
        qkv_s = proj[bs:, :3 * ATTN_WIDTH].reshape(DB, T, 9 * H, HEAD_DIM)
        attn_s = _attn_sample(qkv_s, [c[l] for c in k_caches], [c[l] for c in v_caches], DB, T)
        attn = jnp.concatenate([attn_p, attn_s.reshape(dbt, ATTN_OUT).astype(_BF)], axis=0)

        act_p, tail_p = _conv_prompt(proj, w_dw[l], b_dw[l], g_conv_ln[l], b_conv_ln[l], B, S, glu_off)
        glu_s = proj[bs:, glu_off:gate_off].reshape(DB, T, 2 * C)
        act_s, u_s = _conv_sample(glu_s, state_conv[l], w_dw[l], b_dw[l], g_conv_ln[l], b_conv_ln[l])
        act = jnp.concatenate([act_p, act_s.reshape(dbt, C).astype(_BF)], axis=0)

        z = _branch_merge(attn, w_proj_a[l], act, w_pw2[l], b_pw2[l], proj, gate_off)
        y = _matmul(z, w_o[l], _F32, "out_proj")
        x1, h2, h2p = _tokenwise(_post_mix_body, [x, y], [2, 3, 4], [vec(g_post_mix[l]), vec(g_pre_ffn[l])],
                                 [(_F32, D, False), (_F32, D, False), (jnp.uint32, D // 2, False)], mod, dims)
        f = _moe(h2, h2p, w_router[l], b_router[l], w_gate_up[l], b_gate_up[l], w_down[l], b_down[l])
        (x,) = _tokenwise(_post_ffn_body, [x1, f], [5], [vec(g_post_ffn[l])], [(_F32, D, True)], mod, dims)

        st_p, st_s = [], []
        proj_p = proj[:bs].reshape(B, S, -1)
        for g, (win, _) in enumerate(DILATED_GROUPS):
            for part, caches in ((1, k_caches), (2, v_caches)):
                lo = (part * N_GROUPS + g) * ATTN_OUT
                keep = min(win, S)
                st_p.append(proj_p[:, S - keep:, lo:lo + ATTN_OUT].reshape(B, keep, H, HEAD_DIM))
                new = proj[bs:, lo:lo + ATTN_OUT].reshape(DB, T, H, HEAD_DIM)
                n_past = caches[g].shape[2]
                keep_s = min(win, n_past + T)
                st_s.append(jnp.concatenate([caches[g][l], new], axis=1)[:, n_past + T - keep_s:])
        st_p.append(tail_p[:, CONV_HALO - (CONV_WIDTH - 1):])
        st_s.append(jnp.concatenate([state_conv[l], u_s], axis=1)[:, T:])
        prompt_states.append(st_p)
        sample_states.append(st_s)

    outs_p = [jnp.stack(s, axis=0) for s in zip(*prompt_states)]
    outs_s = [jnp.stack(s, axis=0) for s in zip(*sample_states)]
    return (x[0].reshape(B, S, D), x[1].reshape(DB, T, D), *outs_p, *outs_s)
```

```python
import functools

import jax
import jax.numpy as jnp
from jax import lax
from jax.experimental import pallas as pl
from jax.experimental.pallas import tpu as pltpu

HEAD_DIM = 128
HEADS_PER_GROUP = 8
DILATED_GROUPS = ((128, 1), (512, 4), (2048, 16))
N_GROUPS = len(DILATED_GROUPS)
ATTN_WIDTH = N_GROUPS * HEADS_PER_GROUP * HEAD_DIM
ATTN_OUT = HEADS_PER_GROUP * HEAD_DIM
ATTN_SCALE = HEAD_DIM ** -0.5
ROT_DIM = HEAD_DIM // 4
ROPE_THETA = 500000.0
CONV_WIDTH = 31
PAST_LEN = 8192
TOP_K = 4
SWIGLU_LIMIT = 7.0
SWIGLU_ALPHA = 1.702
RMS_EPS = 1e-6
LN_EPS = 1e-5

LANES = 128
SUBLANES = 8
BF16_ROWS = 16
ATTN_BLOCK = 128
CONV_HALO = 32
MOE_ROW_TILE = 256
VMEM_LIMIT = 56 * 1024 * 1024

_BF = jnp.bfloat16
_F32 = jnp.float32


def _pick(n, target, align):
    best = None
    for t in range(align, min(n, target) + 1, align):
        if n % t == 0:
            best = t
    return best if best is not None else n


def _params(sem):
    return pltpu.CompilerParams(dimension_semantics=sem, vmem_limit_bytes=VMEM_LIMIT)


def _rms(x, g):
    return x * lax.rsqrt(jnp.mean(x * x, axis=-1, keepdims=True) + RMS_EPS) * g


def _adaln_body(c_ref, w_ref, b_ref, o_ref):
    c = c_ref[...]
    a = (c * jax.nn.sigmoid(c)).astype(_BF)
    o_ref[...] = jnp.dot(a, w_ref[...].astype(_BF), preferred_element_type=_F32) + b_ref[...]


def _adaln(c_pad, w, b):
    rows, d = c_pad.shape
    n = w.shape[1]
    tn = _pick(n, 512, LANES)
    return pl.pallas_call(
        _adaln_body,
        grid=(n // tn,),
        in_specs=[pl.BlockSpec((rows, d), lambda j: (0, 0)),
                  pl.BlockSpec((d, tn), lambda j: (0, j)),
                  pl.BlockSpec((1, tn), lambda j: (0, j))],
        out_specs=pl.BlockSpec((rows, tn), lambda j: (0, j)),
        out_shape=jax.ShapeDtypeStruct((rows, n), _F32),
        compiler_params=_params(("parallel",)),
        name="adaln",
    )(c_pad, w, b.reshape(1, n))


def _tokenwise(body, rows_in, mods, gains, outs, mod, dims):
    B, S, DB, T, D = dims
    bs, dbt = B * S, DB * T
    tr = dbt
    n_p = bs // tr
    per_seq = S // tr
    last_p = lambda i: jnp.minimum(i, n_p - 1)

    in_specs, args, pick_p, pick_s = [], [], [], []

    def add(spec, arr):
        in_specs.append(spec)
        args.append(arr)
        return len(args) - 1

    for r in rows_in:
        if isinstance(r, tuple):
            pick_p.append(add(pl.BlockSpec((tr, r[0].shape[1]), lambda i: (last_p(i), 0)), r[0]))
            pick_s.append(add(pl.BlockSpec((tr, r[1].shape[1]), lambda i: (0, 0)), r[1]))
        else:
            k = add(pl.BlockSpec((tr, r.shape[1]), lambda i: (i, 0)), r)
            pick_p.append(k)
            pick_s.append(k)
    mod_p = mod[:B].reshape(B, 1, mod.shape[1])
    mod_s = jnp.repeat(mod[B:B + DB], T, axis=0)
    for w in mods:
        pick_p.append(add(pl.BlockSpec((None, 1, D), lambda i, w=w: (last_p(i) // per_seq, 0, w)), mod_p))
        pick_s.append(add(pl.BlockSpec((tr, D), lambda i, w=w: (0, w)), mod_s))
    for g in gains:
        k = add(pl.BlockSpec((1, D), lambda i: (0, 0)), g)
        pick_p.append(k)
        pick_s.append(k)
    n_in = len(args)
    out_specs, out_shape, out_p, out_s = [], [], [], []
    for dt, width, split in outs:
        if split:
            out_specs.append(pl.BlockSpec((tr, width), lambda i: (last_p(i), 0)))
            out_shape.append(jax.ShapeDtypeStruct((bs, width), dt))
            out_p.append(len(out_specs) - 1)
            out_specs.append(pl.BlockSpec((tr, width), lambda i: (0, 0)))
            out_shape.append(jax.ShapeDtypeStruct((dbt, width), dt))
            out_s.append(len(out_specs) - 1)
        else:
            out_specs.append(pl.BlockSpec((tr, width), lambda i: (i, 0)))
            out_shape.append(jax.ShapeDtypeStruct((bs + dbt, width), dt))
            out_p.append(len(out_specs) - 1)
            out_s.append(len(out_specs) - 1)

    def both(*refs):
        i = pl.program_id(0)

        @pl.when(i < n_p)
        def _():
            body(*[refs[k] for k in pick_p], *[refs[n_in + k] for k in out_p])

        @pl.when(i == n_p)
        def _():
            body(*[refs[k] for k in pick_s], *[refs[n_in + k] for k in out_s])

    res = pl.pallas_call(
        both, grid=(n_p + 1,), in_specs=in_specs, out_specs=out_specs, out_shape=out_shape,
        compiler_params=_params(("arbitrary",)), name=body.__name__.strip("_"),
    )(*args)
    return [(res[out_p[k]], res[out_s[k]]) if outs[k][2] else res[out_p[k]] for k in range(len(outs))]


def _pre_mix_body(x_ref, shift_ref, scale_ref, g_ref, h_ref):
    h = _rms(x_ref[...], g_ref[...]) * (1.0 + scale_ref[...]) + shift_ref[...]
    h_ref[...] = h.astype(_BF)


def _pack_bf16_pair(h):
    half = h.shape[1] // 2
    lo = pltpu.bitcast(h[:, :half].astype(_BF).astype(_F32), jnp.uint32) >> 16
    hi = pltpu.bitcast(h[:, half:].astype(_BF).astype(_F32), jnp.uint32) & jnp.uint32(0xFFFF0000)
    return hi | lo


def _post_mix_body(x_ref, y_ref, gate_ref, shift_ref, scale_ref, gpost_ref, gpre_ref, x1_ref, h2_ref, h2p_ref):
    x1 = x_ref[...] + gate_ref[...] * _rms(y_ref[...], gpost_ref[...])
    x1_ref[...] = x1
    h2 = _rms(x1, gpre_ref[...]) * (1.0 + scale_ref[...]) + shift_ref[...]
    h2_ref[...] = h2
    h2p_ref[...] = _pack_bf16_pair(h2)


def _post_ffn_body(x1_ref, f_ref, gate_ref, gpost_ref, o_ref):
    o_ref[...] = x1_ref[...] + gate_ref[...] * _rms(f_ref[...], gpost_ref[...])


def _mm_body(a_ref, w_ref, o_ref, acc_ref):
    k = pl.program_id(2)

    @pl.when(k == 0)
    def _():
        acc_ref[...] = jnp.zeros_like(acc_ref)

    acc_ref[...] += jnp.dot(a_ref[...], w_ref[...].astype(_BF), preferred_element_type=_F32)

    @pl.when(k == pl.num_programs(2) - 1)
    def _():
        o_ref[...] = acc_ref[...].astype(o_ref.dtype)


def _matmul(a, w, out_dtype, name):
    m, kd = a.shape
    n = w.shape[1]
    tm = _pick(m, 1664, BF16_ROWS)
    tn = _pick(n, 1024, LANES)
    tk = _pick(kd, 1024, LANES)
    return pl.pallas_call(
        _mm_body,
        grid=(m // tm, n // tn, kd // tk),
        in_specs=[pl.BlockSpec((tm, tk), lambda i, j, k: (i, k)),
                  pl.BlockSpec((tk, tn), lambda i, j, k: (k, j))],
        out_specs=pl.BlockSpec((tm, tn), lambda i, j, k: (i, j)),
        out_shape=jax.ShapeDtypeStruct((m, n), out_dtype),
        scratch_shapes=[pltpu.VMEM((tm, tn), _F32)],
        compiler_params=_params(("parallel", "parallel", "arbitrary")),
        name=name,
    )(a, w)


def _rope_tile(x, c, s_lo, s_hi):
    half = ROT_DIM // 2
    parts = []
    for hb in range(x.shape[1] // HEAD_DIM):
        xh = x[:, hb * HEAD_DIM:(hb + 1) * HEAD_DIM]
        up = pltpu.roll(xh, HEAD_DIM - half, 1)
        dn = pltpu.roll(xh, half, 1)
        parts.append(xh * c + up * s_lo + dn * s_hi)
    return jnp.concatenate(parts, axis=1) if len(parts) > 1 else parts[0]


def _in_proj_body(n_rope_tiles, a_ref, w_ref, c_ref, slo_ref, shi_ref, o_ref, acc_ref):
    j = pl.program_id(1)
    k = pl.program_id(2)
    last = pl.num_programs(2) - 1

    @pl.when(k == 0)
    def _():
        acc_ref[...] = jnp.zeros_like(acc_ref)

    acc_ref[...] += jnp.dot(a_ref[...], w_ref[...].astype(_BF), preferred_element_type=_F32)

    @pl.when((k == last) & (j < n_rope_tiles))
    def _():
        o_ref[...] = _rope_tile(acc_ref[...], c_ref[...], slo_ref[...], shi_ref[...])

    @pl.when((k == last) & (j >= n_rope_tiles))
    def _():
        o_ref[...] = acc_ref[...]


def _in_proj(h, w_in, tables):
    m, d = h.shape
    n = w_in.shape[1]
    tm = _pick(m, 1664, BF16_ROWS)
    tk = _pick(d, 1024, LANES)
    tn = LANES
    for t in range(LANES, 1024 + 1, LANES):
        if n % t == 0 and (2 * ATTN_WIDTH) % t == 0:
            tn = t
    tab_spec = pl.BlockSpec((tm, HEAD_DIM), lambda i, j, k: (i, 0))
    return pl.pallas_call(
        functools.partial(_in_proj_body, 2 * ATTN_WIDTH // tn),
        grid=(m // tm, n // tn, d // tk),
        in_specs=[pl.BlockSpec((tm, tk), lambda i, j, k: (i, k)),
                  pl.BlockSpec((tk, tn), lambda i, j, k: (k, j)),
                  tab_spec, tab_spec, tab_spec],
        out_specs=pl.BlockSpec((tm, tn), lambda i, j, k: (i, j)),
        out_shape=jax.ShapeDtypeStruct((m, n), _F32),
        scratch_shapes=[pltpu.VMEM((tm, tn), _F32)],
        compiler_params=_params(("parallel", "parallel", "arbitrary")),
        name="in_proj",
    )(h, w_in, *tables)


def _rope_tables(B, S, DB, T, past_len):
    half = ROT_DIM // 2
    pos = jnp.concatenate([jnp.tile(jnp.arange(S, dtype=jnp.int32), B),
                           jnp.tile(past_len + jnp.arange(T, dtype=jnp.int32), DB)])
    inv_freq = ROPE_THETA ** (-jnp.arange(half, dtype=_F32) / half)
    ang = pos.astype(_F32)[:, None] * inv_freq[None, :]
    cos, sin = jnp.cos(ang), jnp.sin(ang)
    rows = pos.shape[0]
    pad = HEAD_DIM - ROT_DIM
    c = jnp.concatenate([cos, cos, jnp.ones((rows, pad), _F32)], axis=1)
    s_lo = jnp.concatenate([-sin, jnp.zeros((rows, half + pad), _F32)], axis=1)
    s_hi = jnp.concatenate([jnp.zeros((rows, half), _F32), sin, jnp.zeros((rows, pad), _F32)], axis=1)
    return c, s_lo, s_hi


def _branch_merge_body(a1_ref, w1_ref, a2_ref, w2_ref, b2_ref, ga_ref, gb_ref, o_ref):
    ya = jnp.dot(a1_ref[...], w1_ref[...].astype(_BF), preferred_element_type=_F32)
    yb = jnp.dot(a2_ref[...], w2_ref[...].astype(_BF), preferred_element_type=_F32) + b2_ref[...]
    z = jax.nn.sigmoid(ga_ref[...]) * ya + jax.nn.sigmoid(gb_ref[...]) * yb
    o_ref[...] = z.astype(o_ref.dtype)


def _branch_merge(attn, w_proj_a, act, w_pw2, b_pw2, proj, gate_off):
    m, d = attn.shape[0], w_proj_a.shape[1]
    tm = _pick(m, 832, BF16_ROWS)
    tn = LANES
    for t in (256, 512):
        if d % t == 0 and gate_off % t == 0:
            tn = t
    ga_blk, gb_blk = gate_off // tn, (gate_off + d) // tn
    return pl.pallas_call(
        _branch_merge_body,
        grid=(m // tm, d // tn),
        in_specs=[pl.BlockSpec((tm, attn.shape[1]), lambda i, j: (i, 0)),
                  pl.BlockSpec((w_proj_a.shape[0], tn), lambda i, j: (0, j)),
                  pl.BlockSpec((tm, act.shape[1]), lambda i, j: (i, 0)),
                  pl.BlockSpec((w_pw2.shape[0], tn), lambda i, j: (0, j)),
                  pl.BlockSpec((1, tn), lambda i, j: (0, j)),
                  pl.BlockSpec((tm, tn), lambda i, j: (i, ga_blk + j)),
                  pl.BlockSpec((tm, tn), lambda i, j: (i, gb_blk + j))],
        out_specs=pl.BlockSpec((tm, tn), lambda i, j: (i, j)),
        out_shape=jax.ShapeDtypeStruct((m, d), _BF),
        compiler_params=_params(("parallel", "parallel")),
        name="branch_merge",
    )(attn, w_proj_a, act, w_pw2, b_pw2.reshape(1, d), proj, proj)


def _attn_plan(S):
    blk = ATTN_BLOCK
    chunk = _pick(S, 512, blk)
    plan, off = [], 0
    for win, _ in DILATED_GROUPS:
        width = (win // blk + 1) * blk
        if width <= min(S, 1024):
            plan.append(("window", width, off))
            off += width
        else:
            plan.append(("chunks", chunk))
    return plan, off, chunk


def _attn_scores(q, k, row0, col0, win, dil):
    s = lax.dot_general(q, k, (((1,), (1,)), ((), ())), preferred_element_type=_F32) * ATTN_SCALE
    dist = (row0 - col0) + lax.broadcasted_iota(jnp.int32, s.shape, 0) - lax.broadcasted_iota(jnp.int32, s.shape, 1)
    valid = (dist >= 0) & (dist <= win)
    if dil > 1:
        valid = valid & ((dist & (dil - 1)) == 0)
    return jnp.where(valid, s, -jnp.inf)


def _attn_prompt_body(plan, *refs):
    q_refs, k_refs, v_refs, o_ref, sw_ref, sc_ref = refs[0:3], refs[3:6], refs[6:9], refs[9], refs[10], refs[11]
    blk = ATTN_BLOCK
    S = o_ref.shape[0]
    n_blk = S // blk
    rowmax = lambda s: jnp.max(s, axis=1, keepdims=True)

    def q_block(i, _):
        r0 = pl.multiple_of(i * blk, blk)
        qs = [q_refs[g][pl.ds(r0, blk), :].astype(_BF) for g in range(N_GROUPS)]
        starts = {}
        m = jnp.full((blk, 1), -jnp.inf, _F32)
        for g, (win, dil) in enumerate(DILATED_GROUPS):
            if plan[g][0] == "window":
                _, width, off = plan[g]
                c0 = pl.multiple_of(jnp.clip(i - win // blk, 0, (S - width) // blk) * blk, blk)
                starts[g] = c0
                s = _attn_scores(qs[g], k_refs[g][pl.ds(c0, width), :].astype(_BF), r0, c0, win, dil)
                sw_ref[:, off:off + width] = s
                m = jnp.maximum(m, rowmax(s))
            else:
                ch = plan[g][1]
                first = jnp.maximum(r0 - win, 0) // ch

                def score_chunk(c, m, g=g, win=win, dil=dil, ch=ch):
                    c0 = pl.multiple_of(c * ch, ch)
                    s = _attn_scores(qs[g], k_refs[g][pl.ds(c0, ch), :].astype(_BF), r0, c0, win, dil)
                    sc_ref[c] = s
                    return jnp.maximum(m, rowmax(s))

                m = lax.fori_loop(first, r0 // ch + 1, score_chunk, m)
        l = jnp.zeros((blk, 1), _F32)
        acc = jnp.zeros((blk, HEAD_DIM), _F32)
        for g, (win, dil) in enumerate(DILATED_GROUPS):
            if plan[g][0] == "window":
                _, width, off = plan[g]
                p = jnp.exp(sw_ref[:, off:off + width] - m)
                l = l + jnp.sum(p, axis=1, keepdims=True)
                v = v_refs[g][pl.ds(starts[g], width), :].astype(_BF)
                acc = acc + jnp.dot(p.astype(_BF), v, preferred_element_type=_F32)
            else:
                ch = plan[g][1]
                first = jnp.maximum(r0 - win, 0) // ch

                def value_chunk(c, carry, g=g, ch=ch):
                    l, acc = carry
                    c0 = pl.multiple_of(c * ch, ch)
                    p = jnp.exp(sc_ref[c] - m)
                    v = v_refs[g][pl.ds(c0, ch), :].astype(_BF)
                    return (l + jnp.sum(p, axis=1, keepdims=True),
                            acc + jnp.dot(p.astype(_BF), v, preferred_element_type=_F32))

                l, acc = lax.fori_loop(first, r0 // ch + 1, value_chunk, (l, acc))
        o_ref[pl.ds(r0, blk), :] = (acc / l).astype(o_ref.dtype)
        return 0

    lax.fori_loop(0, n_blk, q_block, 0)


def _attn_prompt(proj, B, S):
    H = HEADS_PER_GROUP
    assert all(dil & (dil - 1) == 0 for _, dil in DILATED_GROUPS) and DILATED_GROUPS[0][1] == 1
    plan, win_width, chunk = _attn_plan(S)
    specs = []
    for part in range(3):
        for g in range(N_GROUPS):
            off = part * N_GROUPS * H + g * H
            specs.append(pl.BlockSpec((S, HEAD_DIM), lambda b, h, off=off: (b, off + h)))
    return pl.pallas_call(
        functools.partial(_attn_prompt_body, plan),
        grid=(B, H),
        in_specs=specs,
        out_specs=pl.BlockSpec((S, HEAD_DIM), lambda b, h: (b, h)),
        out_shape=jax.ShapeDtypeStruct((B * S, ATTN_OUT), _BF),
        scratch_shapes=[pltpu.VMEM((ATTN_BLOCK, max(win_width, LANES)), _F32),
                        pltpu.VMEM((S // chunk, ATTN_BLOCK, chunk), _F32)],
        compiler_params=_params(("parallel", "parallel")),
        name="attn_prompt",
    )(*([proj] * 9))


def _attn_sample_body(T, qkv_ref, *refs):
    kc_refs, vc_refs, o_ref = refs[0:3], refs[3:6], refs[6]
    H = HEADS_PER_GROUP
    qkv = qkv_ref[...]
    for t in range(T):
        scores, values = [], []
        for g, (win, dil) in enumerate(DILATED_GROUPS):
            q = qkv[t, g * H:(g + 1) * H, :]
            k_new = qkv[:, (N_GROUPS + g) * H:(N_GROUPS + g + 1) * H, :]
            v_new = qkv[:, (2 * N_GROUPS + g) * H:(2 * N_GROUPS + g + 1) * H, :]
            rows = win // dil
            if dil == 1:
                k_old, v_old = kc_refs[g][...], vc_refs[g][...]
                idx_old = lax.broadcasted_iota(jnp.int32, (rows, H, 1), 0)
            else:
                k_old, v_old = kc_refs[g][:, t], vc_refs[g][:, t]
                idx_old = t + dil * lax.broadcasted_iota(jnp.int32, (rows, H, 1), 0)
            idx_new = win + lax.broadcasted_iota(jnp.int32, (T, H, 1), 0)
            for kk, vv, idx in ((k_old, v_old, idx_old), (k_new, v_new, idx_new)):
                dist = (win + t) - idx
                valid = (dist >= 0) & (dist <= win) & (lax.rem(dist, dil) == 0)
                s = jnp.sum(kk * q[None], axis=-1, keepdims=True) * ATTN_SCALE
                scores.append(jnp.where(valid, s, -jnp.inf))
                values.append(vv)
        m = functools.reduce(jnp.maximum, [jnp.max(s, axis=0, keepdims=True) for s in scores])
        den = jnp.zeros((1, H, 1), _F32)
        num = jnp.zeros((1, H, HEAD_DIM), _F32)
        for s, vv in zip(scores, values):
            p = jnp.exp(s - m)
            den = den + jnp.sum(p, axis=0, keepdims=True)
            num = num + jnp.sum(p * vv, axis=0, keepdims=True)
        o_ref[pl.ds(t, 1)] = num / den


def _attn_sample(qkv_s, k_caches, v_caches, DB, T):
    H = HEADS_PER_GROUP
    args, specs = [qkv_s], [pl.BlockSpec((None, T, 9 * H, HEAD_DIM), lambda b: (b, 0, 0, 0))]
    for caches in (k_caches, v_caches):
        for g, (win, dil) in enumerate(DILATED_GROUPS):
            c = caches[g]
            assert c.shape[1] == win and win % dil == 0 and (dil == 1 or T <= dil)
            if dil == 1:
                args.append(c)
                specs.append(pl.BlockSpec((None, win, H, HEAD_DIM), lambda b: (b, 0, 0, 0)))
            else:
                args.append(c.reshape(DB, win // dil, dil, H, HEAD_DIM))
                specs.append(pl.BlockSpec((None, win // dil, T, H, HEAD_DIM), lambda b: (b, 0, 0, 0, 0)))
    return pl.pallas_call(
        functools.partial(_attn_sample_body, T),
        grid=(DB,),
        in_specs=specs,
        out_specs=pl.BlockSpec((None, T, H, HEAD_DIM), lambda b: (b, 0, 0, 0)),
        out_shape=jax.ShapeDtypeStruct((DB, T, H, HEAD_DIM), _F32),
        compiler_params=_params(("parallel",)),
        name="attn_sample",
    )(*args)


def _ln_silu(conv, g, b):
    xc = conv - jnp.mean(conv, axis=-1, keepdims=True)
    y = xc * lax.rsqrt(jnp.mean(xc * xc, axis=-1, keepdims=True) + LN_EPS) * g + b
    return y * jax.nn.sigmoid(y)


def _conv_prompt_body(n_piece, *refs):
    ua_refs, ub_refs = refs[:n_piece], refs[n_piece:2 * n_piece]
    wdw_ref, bdw_ref, g_ref, b_ref, act_ref, tail_ref, ext_ref, conv_ref = refs[2 * n_piece:]
    i = pl.program_id(1)
    tt = act_ref.shape[0]
    cw = ua_refs[0].shape[1]
    halo = CONV_HALO

    @pl.when(i == 0)
    def _():
        ext_ref[0:halo, :] = jnp.zeros((halo, ext_ref.shape[1]), _F32)

    for p in range(n_piece):
        ext_ref[halo:halo + tt, p * cw:(p + 1) * cw] = ua_refs[p][...] * jax.nn.sigmoid(ub_refs[p][...])

    rc = _pick(tt, 32, SUBLANES)
    cc = _pick(ext_ref.shape[1], 256, LANES)
    off = halo - (CONV_WIDTH - 1)
    win_rows = rc + halo

    def row_chunk(r, _):
        r0 = pl.multiple_of(r * rc, rc)
        for c0 in range(0, ext_ref.shape[1], cc):
            win = ext_ref[pl.ds(r0, win_rows), c0:c0 + cc]
            acc = jnp.broadcast_to(bdw_ref[:, c0:c0 + cc], (rc, cc))
            for res in range(SUBLANES):
                sh = win if res == 0 else pltpu.roll(win, win_rows - res, 0)
                for a0 in range(0, halo + 1, SUBLANES):
                    w = a0 + res - off
                    if 0 <= w < CONV_WIDTH:
                        acc = acc + sh[a0:a0 + rc] * wdw_ref[w:w + 1, c0:c0 + cc]
            conv_ref[pl.ds(r0, rc), c0:c0 + cc] = acc
        return 0

    lax.fori_loop(0, tt // rc, row_chunk, 0)
    act_ref[...] = _ln_silu(conv_ref[...], g_ref[...], b_ref[...]).astype(act_ref.dtype)
    tail = ext_ref[tt:tt + halo, :]
    tail_ref[...] = tail
    ext_ref[0:halo, :] = tail


def _conv_prompt(proj, w_dw, b_dw, g_ln, b_ln, B, S, glu_off):
    C = w_dw.shape[1]
    tt = _pick(S, 256, BF16_ROWS)
    nt = S // tt
    cw = LANES
    for t in range(LANES, 1024 + 1, LANES):
        if C % t == 0 and glu_off % t == 0:
            cw = t
    n_piece = C // cw
    specs = []
    for half in range(2):
        for p in range(n_piece):
            blk = (glu_off + half * C) // cw + p
            specs.append(pl.BlockSpec((tt, cw), lambda b, i, blk=blk: (b * nt + i, blk)))
    vec = lambda rows: pl.BlockSpec((rows, C), lambda b, i: (0, 0))
    act, tail = pl.pallas_call(
        functools.partial(_conv_prompt_body, n_piece),
        grid=(B, nt),
        in_specs=specs + [vec(CONV_WIDTH), vec(1), vec(1), vec(1)],
        out_specs=[pl.BlockSpec((tt, C), lambda b, i: (b * nt + i, 0)),
                   pl.BlockSpec((None, CONV_HALO, C), lambda b, i: (b, 0, 0))],
        out_shape=[jax.ShapeDtypeStruct((B * S, C), _BF), jax.ShapeDtypeStruct((B, CONV_HALO, C), _F32)],
        scratch_shapes=[pltpu.VMEM((CONV_HALO + tt, C), _F32), pltpu.VMEM((tt, C), _F32)],
        compiler_params=_params(("arbitrary", "arbitrary")),
        name="conv_prompt",
    )(*([proj] * (2 * n_piece)), w_dw, b_dw.reshape(1, C), g_ln.reshape(1, C), b_ln.reshape(1, C))
    return act, tail


def _conv_sample_body(T, glu_ref, state_ref, wst_ref, wu_ref, bdw_ref, g_ref, b_ref, act_ref, u_ref):
    C = u_ref.shape[2]
    glu = glu_ref[...]
    u = glu[:, :, :C] * jax.nn.sigmoid(glu[:, :, C:])
    u_ref[...] = u
    state = state_ref[...]
    for t in range(T):
        conv = jnp.sum(state * wst_ref[t][None], axis=1) + jnp.sum(u * wu_ref[t][None], axis=1) + bdw_ref[...]
        act_ref[:, t, :] = _ln_silu(conv, g_ref[...], b_ref[...])


def _conv_sample(glu_s, state, w_dw, b_dw, g_ln, b_ln):
    DB, T, _ = glu_s.shape
    C = w_dw.shape[1]
    n_st = CONV_WIDTH - 1
    r = jnp.arange(n_st)[None, :] - jnp.arange(T)[:, None]
    w_state = jnp.where((r >= 0)[..., None], w_dw[jnp.clip(r, 0, CONV_WIDTH - 1)], 0.0)
    ru = n_st - jnp.arange(T)[:, None] + jnp.arange(T)[None, :]
    w_new = jnp.where((ru <= n_st)[..., None], w_dw[jnp.clip(ru, 0, CONV_WIDTH - 1)], 0.0)
    full = lambda shape: pl.BlockSpec(shape, lambda i: (0,) * len(shape))
    act, u = pl.pallas_call(
        functools.partial(_conv_sample_body, T),
        grid=(1,),
        in_specs=[full(glu_s.shape), full(state.shape), full(w_state.shape), full(w_new.shape),
                  full((1, C)), full((1, C)), full((1, C))],
        out_specs=[full((DB, T, C)), full((DB, T, C))],
        out_shape=[jax.ShapeDtypeStruct((DB, T, C), _F32), jax.ShapeDtypeStruct((DB, T, C), _F32)],
        compiler_params=_params(("arbitrary",)),
        name="conv_sample",
    )(glu_s, state, w_state, w_new, b_dw.reshape(1, C), g_ln.reshape(1, C), b_ln.reshape(1, C))
    return act, u


def _router_body(n_exp, h_ref, w_ref, b_ref, idx_ref, gate_ref, rank_ref, cnt_ref, carry_ref):
    i = pl.program_id(0)
    tm = h_ref.shape[0]

    @pl.when(i == 0)
    def _():
        carry_ref[...] = jnp.zeros_like(carry_ref)

    logits = jnp.dot(h_ref[...], w_ref[...], preferred_element_type=_F32,
                     precision=lax.Precision.HIGHEST) + b_ref[...]
    lane = lax.broadcasted_iota(jnp.int32, (tm, n_exp), 1)
    lane_f = lane.astype(_F32)
    out_lane = lax.broadcasted_iota(jnp.int32, (tm, LANES), 1)
    work = logits
    vals, idxs = [], []
    for _ in range(TOP_K):
        v = jnp.max(work, axis=1, keepdims=True)
        ix = jnp.min(jnp.where(work == v, lane_f, float(n_exp)), axis=1, keepdims=True).astype(jnp.int32)
        vals.append(v)
        idxs.append(ix)
        work = jnp.where(lane == ix, -jnp.inf, work)
    exps = [jnp.exp(v - vals[0]) for v in vals]
    den = functools.reduce(jnp.add, exps)
    onehot = functools.reduce(jnp.add, [(lane == ix).astype(_F32) for ix in idxs])
    tri = (lax.broadcasted_iota(jnp.int32, (tm, tm), 1) < lax.broadcasted_iota(jnp.int32, (tm, tm), 0)).astype(_BF)
    before = jnp.dot(tri, onehot.astype(_BF), preferred_element_type=_F32) + carry_ref[...]
    idx_out = jnp.zeros((tm, LANES), jnp.int32)
    gate_out = jnp.zeros((tm, LANES), _F32)
    rank_out = jnp.zeros((tm, LANES), jnp.int32)
    for k in range(TOP_K):
        rk = jnp.sum(jnp.where(lane == idxs[k], before, 0.0), axis=1, keepdims=True).astype(jnp.int32)
        idx_out = jnp.where(out_lane == k, idxs[k], idx_out)
        gate_out = jnp.where(out_lane == k, exps[k] / den, gate_out)
        rank_out = jnp.where(out_lane == k, rk, rank_out)
    idx_ref[...] = idx_out
    gate_ref[...] = gate_out
    rank_ref[...] = rank_out
    carry_ref[...] += jnp.sum(onehot, axis=0, keepdims=True)
    cnt_ref[...] = carry_ref[...].astype(jnp.int32)


def _router(h2, w_router, b_router):
    m, d = h2.shape
    n_exp = w_router.shape[1]
    tm = _pick(m, 640, SUBLANES)
    row = lambda width: pl.BlockSpec((tm, width), lambda i: (i, 0))
    return pl.pallas_call(
        functools.partial(_router_body, n_exp),
        grid=(m // tm,),
        in_specs=[row(d), pl.BlockSpec((d, n_exp), lambda i: (0, 0)), pl.BlockSpec((1, n_exp), lambda i: (0, 0))],
        out_specs=[row(LANES), row(LANES), row(LANES), pl.BlockSpec((1, n_exp), lambda i: (0, 0))],
        out_shape=[jax.ShapeDtypeStruct((m, LANES), jnp.int32), jax.ShapeDtypeStruct((m, LANES), _F32),
                   jax.ShapeDtypeStruct((m, LANES), jnp.int32), jax.ShapeDtypeStruct((1, n_exp), jnp.int32)],
        scratch_shapes=[pltpu.VMEM((1, n_exp), _F32)],
        compiler_params=_params(("arbitrary",)),
        name="router",
    )(h2, w_router, b_router.reshape(1, n_exp))


def _moe_layout(m, n_exp):
    n_assign = m * TOP_K
    row_tile = MOE_ROW_TILE
    cap = -(-(n_assign * 3 // 2) // (n_exp * row_tile)) * row_tile
    n_chunks = n_exp + n_assign // cap
    return cap, row_tile, n_chunks


def _moe_plan(idx, rank, counts, cap, row_tile, n_chunks):
    n_exp = counts.shape[0]
    n_assign = idx.size
    i32 = jnp.int32
    chunks_per = (counts + cap - 1) // cap
    chunk_end = jnp.cumsum(chunks_per)
    chunk_start = chunk_end - chunks_per
    n_used = chunk_end[-1]
    dest = (chunk_start[idx] + rank // cap) * cap + rank % cap
    c = jnp.arange(n_chunks, dtype=i32)
    c_exp = jnp.minimum(jnp.searchsorted(chunk_end, c, side="right"), n_exp - 1).astype(i32)
    c_rows = jnp.clip(counts[c_exp] - (c - chunk_start[c_exp]) * cap, 0, cap)
    c_rows = jnp.where(c < n_used, c_rows, 0).astype(i32)

    order = jnp.argsort(idx.reshape(-1), stable=True).astype(i32)
    exp_start = jnp.cumsum(counts) - counts
    tiles_per = (c_rows + row_tile - 1) // row_tile
    tile_end = jnp.cumsum(tiles_per)
    tile_start = tile_end - tiles_per
    n_tiles_used = tile_end[-1]
    n_tiles = n_assign // row_tile + n_chunks
    t = jnp.arange(n_tiles, dtype=i32)
    t_c = jnp.minimum(jnp.searchsorted(tile_end, t, side="right"), n_chunks - 1).astype(i32)
    t_k = t - tile_start[t_c]
    blk = t_c * (cap // row_tile) + t_k
    blk = jnp.where(t < n_tiles_used, blk, blk[jnp.maximum(n_tiles_used - 1, 0)]).astype(i32)
    r = t_k[:, None] * row_tile + jnp.arange(row_tile, dtype=i32)[None, :]
    e = c_exp[t_c]
    pos = exp_start[e][:, None] + (t_c - chunk_start[e])[:, None] * cap + r
    row_ok = (t < n_tiles_used)[:, None] & (r < c_rows[t_c][:, None])
    tok = jnp.where(row_ok, order[jnp.clip(pos, 0, n_assign - 1)] // TOP_K, 0).astype(i32)
    gather = (blk, n_tiles_used.astype(i32).reshape(1), tok)
    return dest.astype(i32), (c_exp, c_rows, n_used.astype(i32).reshape(1)), gather


def _dispatch_body(blk_ref, n_tiles_ref, tok_ref, h_ref, o_ref, sem):
    t = pl.program_id(0)

    def row_copy(r):
        return pltpu.make_async_copy(h_ref.at[pl.ds(tok_ref[0, r], 1)], o_ref.at[pl.ds(r, 1)], sem)

    @pl.when(t < n_tiles_ref[0])
    def _():
        def issue(r, _):
            row_copy(r).start()
            return 0

        def drain(r, _):
            row_copy(r).wait()
            return 0

        lax.fori_loop(0, o_ref.shape[0], issue, 0)
        lax.fori_loop(0, o_ref.shape[0], drain, 0)


def _dispatch(h2p, gather, cap, row_tile, n_chunks):
    blk, n_tiles_used, tok = gather
    half = h2p.shape[1]
    n_tiles = tok.shape[0]
    return pl.pallas_call(
        _dispatch_body,
        grid_spec=pltpu.PrefetchScalarGridSpec(
            num_scalar_prefetch=2,
            grid=(n_tiles,),
            in_specs=[pl.BlockSpec((None, 1, row_tile), lambda t, blk, nt: (t, 0, 0), memory_space=pltpu.SMEM),
                      pl.BlockSpec(memory_space=pl.ANY)],
            out_specs=pl.BlockSpec((row_tile, half), lambda t, blk, nt: (blk[t], 0)),
            scratch_shapes=[pltpu.SemaphoreType.DMA(())]),
        out_shape=jax.ShapeDtypeStruct((n_chunks * cap, half), jnp.uint32),
        compiler_params=_params(("arbitrary",)),
        name="moe_dispatch",
    )(blk, n_tiles_used, tok.reshape(n_tiles, 1, row_tile), h2p)


def _chunk_rows_loop(rows, cap, row_tile, tile, o_ref):
    half = row_tile // 2
    n_half = (rows + half - 1) // half
    n_full = n_half // 2

    def full(t, _):
        tile(pl.multiple_of(t * row_tile, row_tile), row_tile)
        return 0

    lax.fori_loop(0, n_full, full, 0)

    @pl.when(n_half % 2 == 1)
    def _():
        tile(pl.multiple_of(n_full * row_tile, row_tile), half)

    def clear(t, _):
        r0 = pl.multiple_of(t * half, half)
        o_ref[pl.ds(r0, half), :] = jnp.zeros((half, o_ref.shape[1]), o_ref.dtype)
        return 0

    lax.fori_loop(n_half, cap // half, clear, 0)


def _unpack_bf16_pair(xp):
    lo = pltpu.bitcast(xp << 16, _F32).astype(_BF)
    hi = pltpu.bitcast(xp & jnp.uint32(0xFFFF0000), _F32).astype(_BF)
    return lo, hi


def _wdot(x, w):
    return lax.dot_general(x, w, (((1,), (0,)), ((), ())), preferred_element_type=_F32)


def _expert_up_body(row_tile, c_exp_ref, c_rows_ref, n_used_ref, x_ref, wg_ref, wu_ref, bg_ref, bu_ref, o_ref):
    c = pl.program_id(0)
    cap, half = x_ref.shape

    @pl.when(c < n_used_ref[0])
    def _():
        def tile(r0, rows):
            lo, hi = _unpack_bf16_pair(x_ref[pl.ds(r0, rows), :])
            g = _wdot(lo, wg_ref[:half, :]) + _wdot(hi, wg_ref[half:, :]) + bg_ref[...]
            u = _wdot(lo, wu_ref[:half, :]) + _wdot(hi, wu_ref[half:, :]) + bu_ref[...]
            g = jnp.minimum(g, SWIGLU_LIMIT)
            u = jnp.clip(u, -SWIGLU_LIMIT, SWIGLU_LIMIT)
            act = (u + 1.0) * (g * jax.nn.sigmoid(SWIGLU_ALPHA * g))
            o_ref[pl.ds(r0, rows), :] = act.astype(o_ref.dtype)

        _chunk_rows_loop(c_rows_ref[c], cap, row_tile, tile, o_ref)


def _chunk_maps(nj):
    def cc(c, nu):
        return jnp.minimum(c, nu[0] - 1)

    def jj(c, j, nu):
        return jnp.where(c < nu[0], j, nj - 1)

    return cc, jj


def _expert_up(xs, w_gate_up, b_gate_up, plan, cap, row_tile, n_chunks):
    c_exp, c_rows, n_used = plan
    n_exp, d, f2 = w_gate_up.shape
    f = f2 // 2
    tn = _pick(f, 256, LANES)
    nj = f // tn
    cc, jj = _chunk_maps(nj)
    b3 = b_gate_up.reshape(n_exp, 1, f2)
    return pl.pallas_call(
        functools.partial(_expert_up_body, row_tile),
        grid_spec=pltpu.PrefetchScalarGridSpec(
            num_scalar_prefetch=3,
            grid=(n_chunks, nj),
            in_specs=[
                pl.BlockSpec((cap, d // 2), lambda c, j, ce, cr, nu: (cc(c, nu), 0)),
                pl.BlockSpec((None, d, tn), lambda c, j, ce, cr, nu: (ce[cc(c, nu)], 0, jj(c, j, nu))),
                pl.BlockSpec((None, d, tn), lambda c, j, ce, cr, nu: (ce[cc(c, nu)], 0, nj + jj(c, j, nu))),
                pl.BlockSpec((None, 1, tn), lambda c, j, ce, cr, nu: (ce[cc(c, nu)], 0, jj(c, j, nu))),
                pl.BlockSpec((None, 1, tn), lambda c, j, ce, cr, nu: (ce[cc(c, nu)], 0, nj + jj(c, j, nu))),
            ],
            out_specs=pl.BlockSpec((cap, tn), lambda c, j, ce, cr, nu: (cc(c, nu), jj(c, j, nu)))),
        out_shape=jax.ShapeDtypeStruct((n_chunks * cap, f), _BF),
        compiler_params=_params(("arbitrary", "arbitrary")),
        name="expert_up",
    )(c_exp, c_rows, n_used, xs, w_gate_up, w_gate_up, b3, b3)


def _expert_down_body(row_tile, c_exp_ref, c_rows_ref, n_used_ref, a_ref, w_ref, b_ref, o_ref):
    c = pl.program_id(0)
    cap = a_ref.shape[0]

    @pl.when(c < n_used_ref[0])
    def _():
        def tile(r0, rows):
            o_ref[pl.ds(r0, rows), :] = _wdot(a_ref[pl.ds(r0, rows), :], w_ref[...]) + b_ref[...]

        _chunk_rows_loop(c_rows_ref[c], cap, row_tile, tile, o_ref)


def _expert_down(act, w_down, b_down, plan, cap, row_tile, n_chunks):
    c_exp, c_rows, n_used = plan
    n_exp, f, d = w_down.shape
    tn = _pick(d, 512, LANES)
    nj = d // tn
    cc, jj = _chunk_maps(nj)
    return pl.pallas_call(
        functools.partial(_expert_down_body, row_tile),
        grid_spec=pltpu.PrefetchScalarGridSpec(
            num_scalar_prefetch=3,
            grid=(n_chunks, nj),
            in_specs=[
                pl.BlockSpec((cap, f), lambda c, j, ce, cr, nu: (cc(c, nu), 0)),
                pl.BlockSpec((None, f, tn), lambda c, j, ce, cr, nu: (ce[cc(c, nu)], 0, jj(c, j, nu))),
                pl.BlockSpec((None, 1, tn), lambda c, j, ce, cr, nu: (ce[cc(c, nu)], 0, jj(c, j, nu))),
            ],
            out_specs=pl.BlockSpec((cap, tn), lambda c, j, ce, cr, nu: (cc(c, nu), jj(c, j, nu)))),
        out_shape=jax.ShapeDtypeStruct((n_chunks * cap, d), _F32),
        compiler_params=_params(("arbitrary", "arbitrary")),
        name="expert_down",
    )(c_exp, c_rows, n_used, act, w_down, b_down.reshape(n_exp, 1, d))


def _combine_body(dest_ref, gate_ref, y_ref, o_ref, buf_ref, sem):
    tc = o_ref.shape[0]

    def row_copy(t, k):
        return pltpu.make_async_copy(y_ref.at[pl.ds(dest_ref[0, t * TOP_K + k], 1)],
                                     buf_ref.at[k, pl.ds(t, 1)], sem)

    def issue(t, _):
        for k in range(TOP_K):
            row_copy(t, k).start()
        return 0

    def drain(t, _):
        for k in range(TOP_K):
            row_copy(t, k).wait()
        return 0

    lax.fori_loop(0, tc, issue, 0)
    lax.fori_loop(0, tc, drain, 0)
    gate = gate_ref[...]
    acc = buf_ref[0] * gate[:, 0:1]
    for k in range(1, TOP_K):
        acc = acc + buf_ref[k] * gate[:, k:k + 1]
    o_ref[...] = acc


def _combine(y, dest, gate):
    m = gate.shape[0]
    d = y.shape[1]
    tc = _pick(m, 128, SUBLANES)
    n_tiles = m // tc
    return pl.pallas_call(
        _combine_body,
        grid=(n_tiles,),
        in_specs=[pl.BlockSpec((None, 1, tc * TOP_K), lambda i: (i, 0, 0), memory_space=pltpu.SMEM),
                  pl.BlockSpec((tc, LANES), lambda i: (i, 0)),
                  pl.BlockSpec(memory_space=pl.ANY)],
        out_specs=pl.BlockSpec((tc, d), lambda i: (i, 0)),
        out_shape=jax.ShapeDtypeStruct((m, d), _F32),
        scratch_shapes=[pltpu.VMEM((TOP_K, tc, d), _F32), pltpu.SemaphoreType.DMA(())],
        compiler_params=_params(("arbitrary",)),
        name="moe_combine",
    )(dest.reshape(n_tiles, 1, tc * TOP_K), gate, y)


def _moe(h2, h2p, w_router, b_router, w_gate_up, b_gate_up, w_down, b_down):
    m = h2.shape[0]
    n_exp = w_router.shape[1]
    idx, gate, rank, counts = _router(h2, w_router, b_router)
    cap, row_tile, n_chunks = _moe_layout(m, n_exp)
    dest, plan, gather = _moe_plan(idx[:, :TOP_K], rank[:, :TOP_K], counts[0], cap, row_tile, n_chunks)
    xs = _dispatch(h2p, gather, cap, row_tile, n_chunks)
    act = _expert_up(xs, w_gate_up, b_gate_up, plan, cap, row_tile, n_chunks)
    y = _expert_down(act, w_down, b_down, plan, cap, row_tile, n_chunks)
    return _combine(y, dest, gate)


def kernel(x_prompt, x_sample, c_prompt, c_sample, cache_k_w128, cache_v_w128, cache_k_w512, cache_v_w512,
           cache_k_w2048, cache_v_w2048, state_conv, w_ada, b_ada, g_pre_mix, g_post_mix, g_pre_ffn, g_post_ffn,
           w_in, w_proj_a, w_dw, b_dw, g_conv_ln, b_conv_ln, w_pw2, b_pw2, w_o, w_router, b_router,
           w_gate_up, b_gate_up, w_down, b_down):
    B, S, D = x_prompt.shape
    DB, T, _ = x_sample.shape
    depth = w_ada.shape[0]
    C = w_dw.shape[2]
    H = HEADS_PER_GROUP
    bs, dbt = B * S, DB * T
    assert S % dbt == 0 and dbt % BF16_ROWS == 0 and S % ATTN_BLOCK == 0 and S >= CONV_HALO
    dims = (B, S, DB, T, D)
    k_caches = (cache_k_w128, cache_k_w512, cache_k_w2048)
    v_caches = (cache_v_w128, cache_v_w512, cache_v_w2048)
    glu_off = 3 * ATTN_WIDTH
    gate_off = glu_off + 2 * C
    tables = _rope_tables(B, S, DB, T, PAST_LEN)
    n_seq = B + DB
    c_all = jnp.concatenate([c_prompt, c_sample], axis=0)
    c_pad = jnp.pad(c_all, ((0, -n_seq % SUBLANES), (0, 0)))
    vec = lambda v: v.reshape(1, -1)

    x = (x_prompt.reshape(bs, D), x_sample.reshape(dbt, D))
    prompt_states, sample_states = [], []
    for l in range(depth):
        mod = _adaln(c_pad, w_ada[l], b_ada[l])
        (h,) = _tokenwise(_pre_mix_body, [x], [0, 1], [vec(g_pre_mix[l])], [(_BF, D, False)], mod, dims)
        proj = _in_proj(h, w_in[l], tables)

        attn_p = _attn_prompt(proj, B, S)
        qkv_s = proj[bs:, :3 * ATTN_WIDTH].reshape(DB, T, 9 * H, HEAD_DIM)
        attn_s = _attn_sample(qkv_s, [c[l] for c in k_caches], [c[l] for c in v_caches], DB, T)
        attn = jnp.concatenate([attn_p, attn_s.reshape(dbt, ATTN_OUT).astype(_BF)], axis=0)

        act_p, tail_p = _conv_prompt(proj, w_dw[l], b_dw[l], g_conv_ln[l], b_conv_ln[l], B, S, glu_off)
        glu_s = proj[bs:, glu_off:gate_off].reshape(DB, T, 2 * C)
        act_s, u_s = _conv_sample(glu_s, state_conv[l], w_dw[l], b_dw[l], g_conv_ln[l], b_conv_ln[l])
        act = jnp.concatenate([act_p, act_s.reshape(dbt, C).astype(_BF)], axis=0)

        z = _branch_merge(attn, w_proj_a[l], act, w_pw2[l], b_pw2[l], proj, gate_off)
        y = _matmul(z, w_o[l], _F32, "out_proj")
        x1, h2, h2p = _tokenwise(_post_mix_body, [x, y], [2, 3, 4], [vec(g_post_mix[l]), vec(g_pre_ffn[l])],
                                 [(_F32, D, False), (_F32, D, False), (jnp.uint32, D // 2, False)], mod, dims)
        f = _moe(h2, h2p, w_router[l], b_router[l], w_gate_up[l], b_gate_up[l], w_down[l], b_down[l])
        (x,) = _tokenwise(_post_ffn_body, [x1, f], [5], [vec(g_post_ffn[l])], [(_F32, D, True)], mod, dims)

        st_p, st_s = [], []
        for g, (win, _) in enumerate(DILATED_GROUPS):
            for part, caches in ((1, k_caches), (2, v_caches)):
                lo = (part * N_GROUPS + g) * ATTN_OUT
                keep = min(win, S)
                cols = lax.slice(proj, (0, lo), (bs, lo + ATTN_OUT)).reshape(B, S, H, HEAD_DIM)
                st_p.append(cols[:, S - keep:])
                new = proj[bs:, lo:lo + ATTN_OUT].reshape(DB, T, H, HEAD_DIM)
                n_past = caches[g].shape[2]
                keep_s = min(win, n_past + T)
                st_s.append(jnp.concatenate([caches[g][l], new], axis=1)[:, n_past + T - keep_s:])
        st_p.append(tail_p[:, CONV_HALO - (CONV_WIDTH - 1):])
        st_s.append(jnp.concatenate([state_conv[l], u_s], axis=1)[:, T:])
        prompt_states.append(st_p)
        sample_states.append(st_s)

    outs_p = [jnp.stack(s, axis=0) for s in zip(*prompt_states)]
    outs_s = [jnp.stack(s, axis=0) for s in zip(*sample_states)]
    return (x[0].reshape(B, S, D), x[1].reshape(DB, T, D), *outs_p, *outs_s)
```

```python
import functools

import jax
import jax.numpy as jnp
from jax import lax
from jax.experimental import pallas as pl
from jax.experimental.pallas import tpu as pltpu

HEAD_DIM = 128
HEADS_PER_GROUP = 8
DILATED_GROUPS = ((128, 1), (512, 4), (2048, 16))
N_GROUPS = len(DILATED_GROUPS)
ATTN_WIDTH = N_GROUPS * HEADS_PER_GROUP * HEAD_DIM
ATTN_OUT = HEADS_PER_GROUP * HEAD_DIM
ATTN_SCALE = HEAD_DIM ** -0.5
ROT_DIM = HEAD_DIM // 4
ROPE_THETA = 500000.0
CONV_WIDTH = 31
PAST_LEN = 8192
TOP_K = 4
SWIGLU_LIMIT = 7.0
SWIGLU_ALPHA = 1.702
RMS_EPS = 1e-6
LN_EPS = 1e-5

LANES = 128
SUBLANES = 8
BF16_ROWS = 16
ATTN_BLOCK = 128
CONV_HALO = 32
MOE_ROW_TILE = 256
DMA_ISSUE_UNROLL = 8
VMEM_LIMIT = 56 * 1024 * 1024

_BF = jnp.bfloat16
_F32 = jnp.float32


def _pick(n, target, align):
    best = None
    for t in range(align, min(n, target) + 1, align):
        if n % t == 0:
            best = t
    return best if best is not None else n


def _params(sem):
    return pltpu.CompilerParams(dimension_semantics=sem, vmem_limit_bytes=VMEM_LIMIT)


def _rms(x, g):
    return x * lax.rsqrt(jnp.mean(x * x, axis=-1, keepdims=True) + RMS_EPS) * g


def _adaln_body(c_ref, w_ref, b_ref, o_ref):
    c = c_ref[...]
    a = (c * jax.nn.sigmoid(c)).astype(_BF)
    o_ref[...] = jnp.dot(a, w_ref[...].astype(_BF), preferred_element_type=_F32) + b_ref[...]


def _adaln(c_pad, w, b):
    rows, d = c_pad.shape
    n = w.shape[1]
    tn = _pick(n, 512, LANES)
    return pl.pallas_call(
        _adaln_body,
        grid=(n // tn,),
        in_specs=[pl.BlockSpec((rows, d), lambda j: (0, 0)),
                  pl.BlockSpec((d, tn), lambda j: (0, j)),
                  pl.BlockSpec((1, tn), lambda j: (0, j))],
        out_specs=pl.BlockSpec((rows, tn), lambda j: (0, j)),
        out_shape=jax.ShapeDtypeStruct((rows, n), _F32),
        compiler_params=_params(("parallel",)),
        name="adaln",
    )(c_pad, w, b.reshape(1, n))


def _tokenwise(body, rows_in, mods, gains, outs, mod, dims):
    B, S, DB, T, D = dims
    bs, dbt = B * S, DB * T
    tr = dbt
    n_p = bs // tr
    per_seq = S // tr
    last_p = lambda i: jnp.minimum(i, n_p - 1)

    in_specs, args, pick_p, pick_s = [], [], [], []

    def add(spec, arr):
        in_specs.append(spec)
        args.append(arr)
        return len(args) - 1

    for r in rows_in:
        if isinstance(r, tuple):
            pick_p.append(add(pl.BlockSpec((tr, r[0].shape[1]), lambda i: (last_p(i), 0)), r[0]))
            pick_s.append(add(pl.BlockSpec((tr, r[1].shape[1]), lambda i: (0, 0)), r[1]))
        else:
            k = add(pl.BlockSpec((tr, r.shape[1]), lambda i: (i, 0)), r)
            pick_p.append(k)
            pick_s.append(k)
    mod_p = mod[:B].reshape(B, 1, mod.shape[1])
    mod_s = jnp.repeat(mod[B:B + DB], T, axis=0)
    for w in mods:
        pick_p.append(add(pl.BlockSpec((None, 1, D), lambda i, w=w: (last_p(i) // per_seq, 0, w)), mod_p))
        pick_s.append(add(pl.BlockSpec((tr, D), lambda i, w=w: (0, w)), mod_s))
    for g in gains:
        k = add(pl.BlockSpec((1, D), lambda i: (0, 0)), g)
        pick_p.append(k)
        pick_s.append(k)
    n_in = len(args)
    out_specs, out_shape, out_p, out_s = [], [], [], []
    for dt, width, split in outs:
        if split:
            out_specs.append(pl.BlockSpec((tr, width), lambda i: (last_p(i), 0)))
            out_shape.append(jax.ShapeDtypeStruct((bs, width), dt))
            out_p.append(len(out_specs) - 1)
            out_specs.append(pl.BlockSpec((tr, width), lambda i: (0, 0)))
            out_shape.append(jax.ShapeDtypeStruct((dbt, width), dt))
            out_s.append(len(out_specs) - 1)
        else:
            out_specs.append(pl.BlockSpec((tr, width), lambda i: (i, 0)))
            out_shape.append(jax.ShapeDtypeStruct((bs + dbt, width), dt))
            out_p.append(len(out_specs) - 1)
            out_s.append(len(out_specs) - 1)

    def both(*refs):
        i = pl.program_id(0)

        @pl.when(i < n_p)
        def _():
            body(*[refs[k] for k in pick_p], *[refs[n_in + k] for k in out_p])

        @pl.when(i == n_p)
        def _():
            body(*[refs[k] for k in pick_s], *[refs[n_in + k] for k in out_s])

    res = pl.pallas_call(
        both, grid=(n_p + 1,), in_specs=in_specs, out_specs=out_specs, out_shape=out_shape,
        compiler_params=_params(("arbitrary",)), name=body.__name__.strip("_"),
    )(*args)
    return [(res[out_p[k]], res[out_s[k]]) if outs[k][2] else res[out_p[k]] for k in range(len(outs))]


def _pre_mix_body(x_ref, shift_ref, scale_ref, g_ref, h_ref):
    h = _rms(x_ref[...], g_ref[...]) * (1.0 + scale_ref[...]) + shift_ref[...]
    h_ref[...] = h.astype(_BF)


def _pack_bf16_pair(h):
    half = h.shape[1] // 2
    lo = pltpu.bitcast(h[:, :half].astype(_BF).astype(_F32), jnp.uint32) >> 16
    hi = pltpu.bitcast(h[:, half:].astype(_BF).astype(_F32), jnp.uint32) & jnp.uint32(0xFFFF0000)
    return hi | lo


def _post_mix_body(x_ref, y_ref, gate_ref, shift_ref, scale_ref, gpost_ref, gpre_ref, x1_ref, h2_ref, h2p_ref):
    x1 = x_ref[...] + gate_ref[...] * _rms(y_ref[...], gpost_ref[...])
    x1_ref[...] = x1
    h2 = _rms(x1, gpre_ref[...]) * (1.0 + scale_ref[...]) + shift_ref[...]
    h2_ref[...] = h2
    h2p_ref[...] = _pack_bf16_pair(h2)


def _post_ffn_body(x1_ref, f_ref, gate_ref, gpost_ref, o_ref):
    o_ref[...] = x1_ref[...] + gate_ref[...] * _rms(f_ref[...], gpost_ref[...])


def _mm_body(a_ref, w_ref, o_ref, acc_ref):
    k = pl.program_id(2)

    @pl.when(k == 0)
    def _():
        acc_ref[...] = jnp.zeros_like(acc_ref)

    acc_ref[...] += jnp.dot(a_ref[...], w_ref[...].astype(_BF), preferred_element_type=_F32)

    @pl.when(k == pl.num_programs(2) - 1)
    def _():
        o_ref[...] = acc_ref[...].astype(o_ref.dtype)


def _matmul(a, w, out_dtype, name):
    m, kd = a.shape
    n = w.shape[1]
    tm = _pick(m, 1664, BF16_ROWS)
    tn = _pick(n, 1024, LANES)
    tk = _pick(kd, 1024, LANES)
    return pl.pallas_call(
        _mm_body,
        grid=(m // tm, n // tn, kd // tk),
        in_specs=[pl.BlockSpec((tm, tk), lambda i, j, k: (i, k)),
                  pl.BlockSpec((tk, tn), lambda i, j, k: (k, j))],
        out_specs=pl.BlockSpec((tm, tn), lambda i, j, k: (i, j)),
        out_shape=jax.ShapeDtypeStruct((m, n), out_dtype),
        scratch_shapes=[pltpu.VMEM((tm, tn), _F32)],
        compiler_params=_params(("parallel", "parallel", "arbitrary")),
        name=name,
    )(a, w)


def _rope_tile(x, c, s_lo, s_hi):
    half = ROT_DIM // 2
    parts = []
    for hb in range(x.shape[1] // HEAD_DIM):
        xh = x[:, hb * HEAD_DIM:(hb + 1) * HEAD_DIM]
        up = pltpu.roll(xh, HEAD_DIM - half, 1)
        dn = pltpu.roll(xh, half, 1)
        parts.append(xh * c + up * s_lo + dn * s_hi)
    return jnp.concatenate(parts, axis=1) if len(parts) > 1 else parts[0]


def _in_proj_body(n_rope_tiles, a_ref, w_ref, c_ref, slo_ref, shi_ref, o_ref):
    j = pl.program_id(1)
    y = _wdot(a_ref[...], w_ref[...])

    @pl.when(j < n_rope_tiles)
    def _():
        o_ref[...] = _rope_tile(y, c_ref[...], slo_ref[...], shi_ref[...])

    @pl.when(j >= n_rope_tiles)
    def _():
        o_ref[...] = y


def _in_proj(h, w_in, tables):
    m, d = h.shape
    n = w_in.shape[1]
    tm = _pick(m, 1664, BF16_ROWS)
    tn = 2 * LANES if n % (2 * LANES) == 0 else LANES
    assert (2 * ATTN_WIDTH) % tn == 0
    tab_spec = pl.BlockSpec((tm, HEAD_DIM), lambda i, j: (i, 0))
    return pl.pallas_call(
        functools.partial(_in_proj_body, 2 * ATTN_WIDTH // tn),
        grid=(m // tm, n // tn),
        in_specs=[pl.BlockSpec((tm, d), lambda i, j: (i, 0)),
                  pl.BlockSpec((d, tn), lambda i, j: (0, j)),
                  tab_spec, tab_spec, tab_spec],
        out_specs=pl.BlockSpec((tm, tn), lambda i, j: (i, j)),
        out_shape=jax.ShapeDtypeStruct((m, n), _F32),
        compiler_params=_params(("parallel", "parallel")),
        name="in_proj",
    )(h, w_in, *tables)


def _rope_tables(B, S, DB, T, past_len):
    half = ROT_DIM // 2
    pos = jnp.concatenate([jnp.tile(jnp.arange(S, dtype=jnp.int32), B),
                           jnp.tile(past_len + jnp.arange(T, dtype=jnp.int32), DB)])
    inv_freq = ROPE_THETA ** (-jnp.arange(half, dtype=_F32) / half)
    ang = pos.astype(_F32)[:, None] * inv_freq[None, :]
    cos, sin = jnp.cos(ang), jnp.sin(ang)
    rows = pos.shape[0]
    pad = HEAD_DIM - ROT_DIM
    c = jnp.concatenate([cos, cos, jnp.ones((rows, pad), _F32)], axis=1)
    s_lo = jnp.concatenate([-sin, jnp.zeros((rows, half + pad), _F32)], axis=1)
    s_hi = jnp.concatenate([jnp.zeros((rows, half), _F32), sin, jnp.zeros((rows, pad), _F32)], axis=1)
    return c, s_lo, s_hi


def _branch_merge_body(a1_ref, w1_ref, a2_ref, w2_ref, b2_ref, ga_ref, gb_ref, o_ref):
    ya = jnp.dot(a1_ref[...], w1_ref[...].astype(_BF), preferred_element_type=_F32)
    yb = jnp.dot(a2_ref[...], w2_ref[...].astype(_BF), preferred_element_type=_F32) + b2_ref[...]
    z = jax.nn.sigmoid(ga_ref[...]) * ya + jax.nn.sigmoid(gb_ref[...]) * yb
    o_ref[...] = z.astype(o_ref.dtype)


def _branch_merge(attn, w_proj_a, act, w_pw2, b_pw2, proj, gate_off):
    m, d = attn.shape[0], w_proj_a.shape[1]
    tm = _pick(m, 832, BF16_ROWS)
    tn = LANES
    for t in (256, 512):
        if d % t == 0 and gate_off % t == 0:
            tn = t
    ga_blk, gb_blk = gate_off // tn, (gate_off + d) // tn
    return pl.pallas_call(
        _branch_merge_body,
        grid=(m // tm, d // tn),
        in_specs=[pl.BlockSpec((tm, attn.shape[1]), lambda i, j: (i, 0)),
                  pl.BlockSpec((w_proj_a.shape[0], tn), lambda i, j: (0, j)),
                  pl.BlockSpec((tm, act.shape[1]), lambda i, j: (i, 0)),
                  pl.BlockSpec((w_pw2.shape[0], tn), lambda i, j: (0, j)),
                  pl.BlockSpec((1, tn), lambda i, j: (0, j)),
                  pl.BlockSpec((tm, tn), lambda i, j: (i, ga_blk + j)),
                  pl.BlockSpec((tm, tn), lambda i, j: (i, gb_blk + j))],
        out_specs=pl.BlockSpec((tm, tn), lambda i, j: (i, j)),
        out_shape=jax.ShapeDtypeStruct((m, d), _BF),
        compiler_params=_params(("parallel", "parallel")),
        name="branch_merge",
    )(attn, w_proj_a, act, w_pw2, b_pw2.reshape(1, d), proj, proj)


def _attn_plan(S):
    blk = ATTN_BLOCK
    chunk = _pick(S, 512, blk)
    plan, off = [], 0
    for win, _ in DILATED_GROUPS:
        width = (win // blk + 1) * blk
        if width <= min(S, 1024):
            plan.append(("window", width, off))
            off += width
        else:
            plan.append(("chunks", chunk))
    return plan, off, chunk


def _attn_scores(q, k, row0, col0, win, dil):
    s = lax.dot_general(q, k, (((1,), (1,)), ((), ())), preferred_element_type=_F32) * ATTN_SCALE
    dist = (row0 - col0) + lax.broadcasted_iota(jnp.int32, s.shape, 0) - lax.broadcasted_iota(jnp.int32, s.shape, 1)
    valid = (dist >= 0) & (dist <= win)
    if dil > 1:
        valid = valid & ((dist & (dil - 1)) == 0)
    return jnp.where(valid, s, -jnp.inf)


def _attn_prompt_body(plan, *refs):
    q_refs, k_refs, v_refs, o_ref, sw_ref, sc_ref = refs[0:3], refs[3:6], refs[6:9], refs[9], refs[10], refs[11]
    blk = ATTN_BLOCK
    S = o_ref.shape[0]
    n_blk = S // blk
    rowmax = lambda s: jnp.max(s, axis=1, keepdims=True)

    def q_block(i, _):
        r0 = pl.multiple_of(i * blk, blk)
        qs = [q_refs[g][pl.ds(r0, blk), :].astype(_BF) for g in range(N_GROUPS)]
        starts = {}
        m = jnp.full((blk, 1), -jnp.inf, _F32)
        for g, (win, dil) in enumerate(DILATED_GROUPS):
            if plan[g][0] == "window":
                _, width, off = plan[g]
                c0 = pl.multiple_of(jnp.clip(i - win // blk, 0, (S - width) // blk) * blk, blk)
                starts[g] = c0
                s = _attn_scores(qs[g], k_refs[g][pl.ds(c0, width), :].astype(_BF), r0, c0, win, dil)
                sw_ref[:, off:off + width] = s
                m = jnp.maximum(m, rowmax(s))
            else:
                ch = plan[g][1]
                first = jnp.maximum(r0 - win, 0) // ch

                def score_chunk(c, m, g=g, win=win, dil=dil, ch=ch):
                    c0 = pl.multiple_of(c * ch, ch)
                    s = _attn_scores(qs[g], k_refs[g][pl.ds(c0, ch), :].astype(_BF), r0, c0, win, dil)
                    sc_ref[c] = s
                    return jnp.maximum(m, rowmax(s))

                m = lax.fori_loop(first, r0 // ch + 1, score_chunk, m)
        l = jnp.zeros((blk, 1), _F32)
        acc = jnp.zeros((blk, HEAD_DIM), _F32)
        for g, (win, dil) in enumerate(DILATED_GROUPS):
            if plan[g][0] == "window":
                _, width, off = plan[g]
                p = jnp.exp(sw_ref[:, off:off + width] - m)
                l = l + jnp.sum(p, axis=1, keepdims=True)
                v = v_refs[g][pl.ds(starts[g], width), :].astype(_BF)
                acc = acc + jnp.dot(p.astype(_BF), v, preferred_element_type=_F32)
            else:
                ch = plan[g][1]
                first = jnp.maximum(r0 - win, 0) // ch

                def value_chunk(c, carry, g=g, ch=ch):
                    l, acc = carry
                    c0 = pl.multiple_of(c * ch, ch)
                    p = jnp.exp(sc_ref[c] - m)
                    v = v_refs[g][pl.ds(c0, ch), :].astype(_BF)
                    return (l + jnp.sum(p, axis=1, keepdims=True),
                            acc + jnp.dot(p.astype(_BF), v, preferred_element_type=_F32))

                l, acc = lax.fori_loop(first, r0 // ch + 1, value_chunk, (l, acc))
        o_ref[pl.ds(r0, blk), :] = (acc / l).astype(o_ref.dtype)
        return 0

    lax.fori_loop(0, n_blk, q_block, 0)


def _attn_prompt(proj, B, S):
    H = HEADS_PER_GROUP
    assert all(dil & (dil - 1) == 0 for _, dil in DILATED_GROUPS) and DILATED_GROUPS[0][1] == 1
    plan, win_width, chunk = _attn_plan(S)
    specs = []
    for part in range(3):
        for g in range(N_GROUPS):
            off = part * N_GROUPS * H + g * H
            specs.append(pl.BlockSpec((S, HEAD_DIM), lambda b, h, off=off: (b, off + h)))
    return pl.pallas_call(
        functools.partial(_attn_prompt_body, plan),
        grid=(B, H),
        in_specs=specs,
        out_specs=pl.BlockSpec((S, HEAD_DIM), lambda b, h: (b, h)),
        out_shape=jax.ShapeDtypeStruct((B * S, ATTN_OUT), _BF),
        scratch_shapes=[pltpu.VMEM((ATTN_BLOCK, max(win_width, LANES)), _F32),
                        pltpu.VMEM((S // chunk, ATTN_BLOCK, chunk), _F32)],
        compiler_params=_params(("parallel", "parallel")),
        name="attn_prompt",
    )(*([proj] * 9))


def _attn_sample_body(T, qkv_ref, *refs):
    kc_refs, vc_refs, o_ref = refs[0:3], refs[3:6], refs[6]
    H = HEADS_PER_GROUP
    qkv = qkv_ref[...]
    for t in range(T):
        scores, values = [], []
        for g, (win, dil) in enumerate(DILATED_GROUPS):
            q = qkv[t, g * H:(g + 1) * H, :]
            k_new = qkv[:, (N_GROUPS + g) * H:(N_GROUPS + g + 1) * H, :]
            v_new = qkv[:, (2 * N_GROUPS + g) * H:(2 * N_GROUPS + g + 1) * H, :]
            rows = win // dil
            if dil == 1:
                k_old, v_old = kc_refs[g][...], vc_refs[g][...]
                idx_old = lax.broadcasted_iota(jnp.int32, (rows, H, 1), 0)
            else:
                k_old, v_old = kc_refs[g][:, t], vc_refs[g][:, t]
                idx_old = t + dil * lax.broadcasted_iota(jnp.int32, (rows, H, 1), 0)
            idx_new = win + lax.broadcasted_iota(jnp.int32, (T, H, 1), 0)
            for kk, vv, idx in ((k_old, v_old, idx_old), (k_new, v_new, idx_new)):
                dist = (win + t) - idx
                valid = (dist >= 0) & (dist <= win) & (lax.rem(dist, dil) == 0)
                s = jnp.sum(kk * q[None], axis=-1, keepdims=True) * ATTN_SCALE
                scores.append(jnp.where(valid, s, -jnp.inf))
                values.append(vv)
        m = functools.reduce(jnp.maximum, [jnp.max(s, axis=0, keepdims=True) for s in scores])
        den = jnp.zeros((1, H, 1), _F32)
        num = jnp.zeros((1, H, HEAD_DIM), _F32)
        for s, vv in zip(scores, values):
            p = jnp.exp(s - m)
            den = den + jnp.sum(p, axis=0, keepdims=True)
            num = num + jnp.sum(p * vv, axis=0, keepdims=True)
        o_ref[pl.ds(t, 1)] = num / den


def _attn_sample(qkv_s, k_caches, v_caches, DB, T):
    H = HEADS_PER_GROUP
    args, specs = [qkv_s], [pl.BlockSpec((None, T, 9 * H, HEAD_DIM), lambda b: (b, 0, 0, 0))]
    for caches in (k_caches, v_caches):
        for g, (win, dil) in enumerate(DILATED_GROUPS):
            c = caches[g]
            assert c.shape[1] == win and win % dil == 0 and (dil == 1 or T <= dil)
            if dil == 1:
                args.append(c)
                specs.append(pl.BlockSpec((None, win, H, HEAD_DIM), lambda b: (b, 0, 0, 0)))
            else:
                args.append(c.reshape(DB, win // dil, dil, H, HEAD_DIM))
                specs.append(pl.BlockSpec((None, win // dil, T, H, HEAD_DIM), lambda b: (b, 0, 0, 0, 0)))
    return pl.pallas_call(
        functools.partial(_attn_sample_body, T),
        grid=(DB,),
        in_specs=specs,
        out_specs=pl.BlockSpec((None, T, H, HEAD_DIM), lambda b: (b, 0, 0, 0)),
        out_shape=jax.ShapeDtypeStruct((DB, T, H, HEAD_DIM), _F32),
        compiler_params=_params(("parallel",)),
        name="attn_sample",
    )(*args)


def _ln_silu(conv, g, b):
    xc = conv - jnp.mean(conv, axis=-1, keepdims=True)
    y = xc * lax.rsqrt(jnp.mean(xc * xc, axis=-1, keepdims=True) + LN_EPS) * g + b
    return y * jax.nn.sigmoid(y)


def _conv_prompt_body(n_piece, *refs):
    ua_refs, ub_refs = refs[:n_piece], refs[n_piece:2 * n_piece]
    wdw_ref, bdw_ref, g_ref, b_ref, act_ref, tail_ref, ext_ref, conv_ref = refs[2 * n_piece:]
    i = pl.program_id(1)
    tt = act_ref.shape[0]
    cw = ua_refs[0].shape[1]
    halo = CONV_HALO

    @pl.when(i == 0)
    def _():
        ext_ref[0:halo, :] = jnp.zeros((halo, ext_ref.shape[1]), _F32)

    for p in range(n_piece):
        ext_ref[halo:halo + tt, p * cw:(p + 1) * cw] = ua_refs[p][...] * jax.nn.sigmoid(ub_refs[p][...])

    rc = _pick(tt, 32, SUBLANES)
    cc = _pick(ext_ref.shape[1], 256, LANES)
    off = halo - (CONV_WIDTH - 1)
    win_rows = rc + halo

    def row_chunk(r, _):
        r0 = pl.multiple_of(r * rc, rc)
        for c0 in range(0, ext_ref.shape[1], cc):
            win = ext_ref[pl.ds(r0, win_rows), c0:c0 + cc]
            acc = jnp.broadcast_to(bdw_ref[:, c0:c0 + cc], (rc, cc))
            for res in range(SUBLANES):
                sh = win if res == 0 else pltpu.roll(win, win_rows - res, 0)
                for a0 in range(0, halo + 1, SUBLANES):
                    w = a0 + res - off
                    if 0 <= w < CONV_WIDTH:
                        acc = acc + sh[a0:a0 + rc] * wdw_ref[w:w + 1, c0:c0 + cc]
            conv_ref[pl.ds(r0, rc), c0:c0 + cc] = acc
        return 0

    lax.fori_loop(0, tt // rc, row_chunk, 0)
    act_ref[...] = _ln_silu(conv_ref[...], g_ref[...], b_ref[...]).astype(act_ref.dtype)
    tail = ext_ref[tt:tt + halo, :]
    tail_ref[...] = tail
    ext_ref[0:halo, :] = tail


def _conv_prompt(proj, w_dw, b_dw, g_ln, b_ln, B, S, glu_off):
    C = w_dw.shape[1]
    tt = _pick(S, 256, BF16_ROWS)
    nt = S // tt
    cw = LANES
    for t in range(LANES, 1024 + 1, LANES):
        if C % t == 0 and glu_off % t == 0:
            cw = t
    n_piece = C // cw
    specs = []
    for half in range(2):
        for p in range(n_piece):
            blk = (glu_off + half * C) // cw + p
            specs.append(pl.BlockSpec((tt, cw), lambda b, i, blk=blk: (b * nt + i, blk)))
    vec = lambda rows: pl.BlockSpec((rows, C), lambda b, i: (0, 0))
    act, tail = pl.pallas_call(
        functools.partial(_conv_prompt_body, n_piece),
        grid=(B, nt),
        in_specs=specs + [vec(CONV_WIDTH), vec(1), vec(1), vec(1)],
        out_specs=[pl.BlockSpec((tt, C), lambda b, i: (b * nt + i, 0)),
                   pl.BlockSpec((None, CONV_HALO, C), lambda b, i: (b, 0, 0))],
        out_shape=[jax.ShapeDtypeStruct((B * S, C), _BF), jax.ShapeDtypeStruct((B, CONV_HALO, C), _F32)],
        scratch_shapes=[pltpu.VMEM((CONV_HALO + tt, C), _F32), pltpu.VMEM((tt, C), _F32)],
        compiler_params=_params(("arbitrary", "arbitrary")),
        name="conv_prompt",
    )(*([proj] * (2 * n_piece)), w_dw, b_dw.reshape(1, C), g_ln.reshape(1, C), b_ln.reshape(1, C))
    return act, tail


def _conv_sample_body(T, glu_ref, state_ref, wst_ref, wu_ref, bdw_ref, g_ref, b_ref, act_ref, u_ref):
    C = u_ref.shape[2]
    glu = glu_ref[...]
    u = glu[:, :, :C] * jax.nn.sigmoid(glu[:, :, C:])
    u_ref[...] = u
    state = state_ref[...]
    for t in range(T):
        conv = jnp.sum(state * wst_ref[t][None], axis=1) + jnp.sum(u * wu_ref[t][None], axis=1) + bdw_ref[...]
        act_ref[:, t, :] = _ln_silu(conv, g_ref[...], b_ref[...])


def _conv_sample(glu_s, state, w_dw, b_dw, g_ln, b_ln):
    DB, T, _ = glu_s.shape
    C = w_dw.shape[1]
    n_st = CONV_WIDTH - 1
    r = jnp.arange(n_st)[None, :] - jnp.arange(T)[:, None]
    w_state = jnp.where((r >= 0)[..., None], w_dw[jnp.clip(r, 0, CONV_WIDTH - 1)], 0.0)
    ru = n_st - jnp.arange(T)[:, None] + jnp.arange(T)[None, :]
    w_new = jnp.where((ru <= n_st)[..., None], w_dw[jnp.clip(ru, 0, CONV_WIDTH - 1)], 0.0)
    full = lambda shape: pl.BlockSpec(shape, lambda i: (0,) * len(shape))
    act, u = pl.pallas_call(
        functools.partial(_conv_sample_body, T),
        grid=(1,),
        in_specs=[full(glu_s.shape), full(state.shape), full(w_state.shape), full(w_new.shape),
                  full((1, C)), full((1, C)), full((1, C))],
        out_specs=[full((DB, T, C)), full((DB, T, C))],
        out_shape=[jax.ShapeDtypeStruct((DB, T, C), _F32), jax.ShapeDtypeStruct((DB, T, C), _F32)],
        compiler_params=_params(("arbitrary",)),
        name="conv_sample",
    )(glu_s, state, w_state, w_new, b_dw.reshape(1, C), g_ln.reshape(1, C), b_ln.reshape(1, C))
    return act, u


def _router_body(n_exp, h_ref, w_ref, b_ref, idx_ref, gate_ref, rank_ref, cnt_ref, carry_ref):
    i = pl.program_id(0)
    tm = h_ref.shape[0]

    @pl.when(i == 0)
    def _():
        carry_ref[...] = jnp.zeros_like(carry_ref)

    logits = jnp.dot(h_ref[...], w_ref[...], preferred_element_type=_F32,
                     precision=lax.Precision.HIGHEST) + b_ref[...]
    lane = lax.broadcasted_iota(jnp.int32, (tm, n_exp), 1)
    lane_f = lane.astype(_F32)
    out_lane = lax.broadcasted_iota(jnp.int32, (tm, LANES), 1)
    work = logits
    vals, idxs = [], []
    for _ in range(TOP_K):
        v = jnp.max(work, axis=1, keepdims=True)
        ix = jnp.min(jnp.where(work == v, lane_f, float(n_exp)), axis=1, keepdims=True).astype(jnp.int32)
        vals.append(v)
        idxs.append(ix)
        work = jnp.where(lane == ix, -jnp.inf, work)
    exps = [jnp.exp(v - vals[0]) for v in vals]
    den = functools.reduce(jnp.add, exps)
    onehot = functools.reduce(jnp.add, [(lane == ix).astype(_F32) for ix in idxs])
    tri = (lax.broadcasted_iota(jnp.int32, (tm, tm), 1) < lax.broadcasted_iota(jnp.int32, (tm, tm), 0)).astype(_BF)
    before = jnp.dot(tri, onehot.astype(_BF), preferred_element_type=_F32) + carry_ref[...]
    idx_out = jnp.zeros((tm, LANES), jnp.int32)
    gate_out = jnp.zeros((tm, LANES), _F32)
    rank_out = jnp.zeros((tm, LANES), jnp.int32)
    for k in range(TOP_K):
        rk = jnp.sum(jnp.where(lane == idxs[k], before, 0.0), axis=1, keepdims=True).astype(jnp.int32)
        idx_out = jnp.where(out_lane == k, idxs[k], idx_out)
        gate_out = jnp.where(out_lane == k, exps[k] / den, gate_out)
        rank_out = jnp.where(out_lane == k, rk, rank_out)
    idx_ref[...] = idx_out
    gate_ref[...] = gate_out
    rank_ref[...] = rank_out
    carry_ref[...] += jnp.sum(onehot, axis=0, keepdims=True)
    cnt_ref[...] = carry_ref[...].astype(jnp.int32)


def _router(h2, w_router, b_router):
    m, d = h2.shape
    n_exp = w_router.shape[1]
    tm = _pick(m, 640, SUBLANES)
    row = lambda width: pl.BlockSpec((tm, width), lambda i: (i, 0))
    return pl.pallas_call(
        functools.partial(_router_body, n_exp),
        grid=(m // tm,),
        in_specs=[row(d), pl.BlockSpec((d, n_exp), lambda i: (0, 0)), pl.BlockSpec((1, n_exp), lambda i: (0, 0))],
        out_specs=[row(LANES), row(LANES), row(LANES), pl.BlockSpec((1, n_exp), lambda i: (0, 0))],
        out_shape=[jax.ShapeDtypeStruct((m, LANES), jnp.int32), jax.ShapeDtypeStruct((m, LANES), _F32),
                   jax.ShapeDtypeStruct((m, LANES), jnp.int32), jax.ShapeDtypeStruct((1, n_exp), jnp.int32)],
        scratch_shapes=[pltpu.VMEM((1, n_exp), _F32)],
        compiler_params=_params(("arbitrary",)),
        name="router",
    )(h2, w_router, b_router.reshape(1, n_exp))


def _moe_layout(m, n_exp):
    n_assign = m * TOP_K
    row_tile = MOE_ROW_TILE
    cap = -(-(n_assign * 3 // 2) // (n_exp * row_tile)) * row_tile
    n_chunks = n_exp + n_assign // cap
    return cap, row_tile, n_chunks


def _moe_plan(idx, rank, counts, cap, row_tile, n_chunks):
    n_exp = counts.shape[0]
    n_assign = idx.size
    i32 = jnp.int32
    chunks_per = (counts + cap - 1) // cap
    chunk_end = jnp.cumsum(chunks_per)
    chunk_start = chunk_end - chunks_per
    n_used = chunk_end[-1]
    dest = (chunk_start[idx] + rank // cap) * cap + rank % cap
    c = jnp.arange(n_chunks, dtype=i32)
    c_exp = jnp.minimum(jnp.searchsorted(chunk_end, c, side="right"), n_exp - 1).astype(i32)
    c_rows = jnp.clip(counts[c_exp] - (c - chunk_start[c_exp]) * cap, 0, cap)
    c_rows = jnp.where(c < n_used, c_rows, 0).astype(i32)

    order = jnp.argsort(idx.reshape(-1), stable=True).astype(i32)
    exp_start = jnp.cumsum(counts) - counts
    tiles_per = (c_rows + row_tile - 1) // row_tile
    tile_end = jnp.cumsum(tiles_per)
    tile_start = tile_end - tiles_per
    n_tiles_used = tile_end[-1]
    n_tiles = n_assign // row_tile + n_chunks
    t = jnp.arange(n_tiles, dtype=i32)
    t_c = jnp.minimum(jnp.searchsorted(tile_end, t, side="right"), n_chunks - 1).astype(i32)
    t_k = t - tile_start[t_c]
    blk = t_c * (cap // row_tile) + t_k
    blk = jnp.where(t < n_tiles_used, blk, blk[jnp.maximum(n_tiles_used - 1, 0)]).astype(i32)
    r = t_k[:, None] * row_tile + jnp.arange(row_tile, dtype=i32)[None, :]
    e = c_exp[t_c]
    pos = exp_start[e][:, None] + (t_c - chunk_start[e])[:, None] * cap + r
    row_ok = (t < n_tiles_used)[:, None] & (r < c_rows[t_c][:, None])
    tok = jnp.where(row_ok, order[jnp.clip(pos, 0, n_assign - 1)] // TOP_K, 0).astype(i32)
    gather = (blk, n_tiles_used.astype(i32).reshape(1), tok)
    return dest.astype(i32), (c_exp, c_rows, n_used.astype(i32).reshape(1)), gather


def _dispatch_body(blk_ref, n_tiles_ref, tok_ref, h_ref, o_ref, sem):
    t = pl.program_id(0)

    def row_copy(r):
        return pltpu.make_async_copy(h_ref.at[pl.ds(tok_ref[0, r], 1)], o_ref.at[pl.ds(r, 1)], sem)

    @pl.when(t < n_tiles_ref[0])
    def _():
        def issue(r, _):
            row_copy(r).start()
            return 0

        lax.fori_loop(0, o_ref.shape[0], issue, 0, unroll=DMA_ISSUE_UNROLL)
        pltpu.make_async_copy(h_ref.at[pl.ds(0, o_ref.shape[0])], o_ref, sem).wait()


def _dispatch(h2p, gather, cap, row_tile, n_chunks):
    blk, n_tiles_used, tok = gather
    half = h2p.shape[1]
    n_tiles = tok.shape[0]
    return pl.pallas_call(
        _dispatch_body,
        grid_spec=pltpu.PrefetchScalarGridSpec(
            num_scalar_prefetch=2,
            grid=(n_tiles,),
            in_specs=[pl.BlockSpec((None, 1, row_tile), lambda t, blk, nt: (t, 0, 0), memory_space=pltpu.SMEM),
                      pl.BlockSpec(memory_space=pl.ANY)],
            out_specs=pl.BlockSpec((row_tile, half), lambda t, blk, nt: (blk[t], 0)),
            scratch_shapes=[pltpu.SemaphoreType.DMA(())]),
        out_shape=jax.ShapeDtypeStruct((n_chunks * cap, half), jnp.uint32),
        compiler_params=_params(("arbitrary",)),
        name="moe_dispatch",
    )(blk, n_tiles_used, tok.reshape(n_tiles, 1, row_tile), h2p)


def _chunk_rows_loop(rows, cap, row_tile, tile, o_ref):
    unit = row_tile // 2
    big = 2 * row_tile
    n_unit = (rows + unit - 1) // unit
    n_big = n_unit // 4
    rem = n_unit % 4

    def full(t, _):
        tile(pl.multiple_of(t * big, big), big)
        return 0

    lax.fori_loop(0, n_big, full, 0)

    @pl.when(rem >= 2)
    def _():
        tile(pl.multiple_of(n_big * big, big), row_tile)

    @pl.when(rem % 2 == 1)
    def _():
        tile(pl.multiple_of(n_big * big + (rem // 2) * row_tile, row_tile), unit)

    def clear(t, _):
        r0 = pl.multiple_of(t * unit, unit)
        o_ref[pl.ds(r0, unit), :] = jnp.zeros((unit, o_ref.shape[1]), o_ref.dtype)
        return 0

    lax.fori_loop(n_unit, cap // unit, clear, 0)


def _unpack_bf16_pair(xp):
    lo = pltpu.bitcast(xp << 16, _F32).astype(_BF)
    hi = pltpu.bitcast(xp & jnp.uint32(0xFFFF0000), _F32).astype(_BF)
    return lo, hi


def _wdot(x, w):
    return lax.dot_general(x, w, (((1,), (0,)), ((), ())), preferred_element_type=_F32)


def _expert_up_body(row_tile, layer, n_copy, c_exp_ref, c_rows_ref, n_used_ref, x_ref, wg_ref, wu_ref, bg_ref,
                    bu_ref, *rest):
    cache_refs, new_refs = rest[:n_copy], rest[n_copy:2 * n_copy]
    o_ref = rest[2 * n_copy]
    state_refs, sems = rest[2 * n_copy + 1:3 * n_copy + 1], rest[3 * n_copy + 1]
    c = pl.program_id(0)
    j = pl.program_id(1)
    cap, half = x_ref.shape

    copies = []
    for i in range(n_copy):
        keep, t_new = state_refs[i].shape[1], new_refs[i].shape[1]
        n_past = cache_refs[i].shape[2]
        copies.append(pltpu.make_async_copy(
            cache_refs[i].at[layer, :, pl.ds(n_past + t_new - keep, keep - t_new)],
            state_refs[i].at[:, pl.ds(0, keep - t_new)], sems.at[2 * i]))
        copies.append(pltpu.make_async_copy(new_refs[i], state_refs[i].at[:, pl.ds(keep - t_new, t_new)],
                                            sems.at[2 * i + 1]))

    @pl.when((c == 0) & (j == 0))
    def _():
        for cp in copies:
            cp.start()

    @pl.when((c == pl.num_programs(0) - 1) & (j == pl.num_programs(1) - 1))
    def _():
        for cp in copies:
            cp.wait()

    @pl.when(c < n_used_ref[0])
    def _():
        def tile(r0, rows):
            lo, hi = _unpack_bf16_pair(x_ref[pl.ds(r0, rows), :])
            g = _wdot(lo, wg_ref[:half, :]) + _wdot(hi, wg_ref[half:, :]) + bg_ref[...]
            u = _wdot(lo, wu_ref[:half, :]) + _wdot(hi, wu_ref[half:, :]) + bu_ref[...]
            g = jnp.minimum(g, SWIGLU_LIMIT)
            u = jnp.clip(u, -SWIGLU_LIMIT, SWIGLU_LIMIT)
            act = (u + 1.0) * (g * jax.nn.sigmoid(SWIGLU_ALPHA * g))
            o_ref[pl.ds(r0, rows), :] = act.astype(o_ref.dtype)

        _chunk_rows_loop(c_rows_ref[c], cap, row_tile, tile, o_ref)


def _chunk_maps(nj):
    def cc(c, nu):
        return jnp.minimum(c, nu[0] - 1)

    def jj(c, j, nu):
        return jnp.where(c < nu[0], j, nj - 1)

    return cc, jj


def _expert_up(xs, w_gate_up, b_gate_up, plan, cap, row_tile, n_chunks, layer, caches, new_rows, keeps):
    c_exp, c_rows, n_used = plan
    n_exp, d, f2 = w_gate_up.shape
    f = f2 // 2
    tn = _pick(f, 256, LANES)
    nj = f // tn
    cc, jj = _chunk_maps(nj)
    b3 = b_gate_up.reshape(n_exp, 1, f2)
    n_copy = len(caches)
    any_spec = pl.BlockSpec(memory_space=pl.ANY)
    state_shapes = []
    for cache, new, keep in zip(caches, new_rows, keeps):
        assert new.shape[1] < keep <= cache.shape[2] + new.shape[1]
        state_shapes.append(jax.ShapeDtypeStruct((new.shape[0], keep) + new.shape[2:], new.dtype))
    res = pl.pallas_call(
        functools.partial(_expert_up_body, row_tile, layer, n_copy),
        grid_spec=pltpu.PrefetchScalarGridSpec(
            num_scalar_prefetch=3,
            grid=(n_chunks, nj),
            in_specs=[
                pl.BlockSpec((cap, d // 2), lambda c, j, ce, cr, nu: (cc(c, nu), 0)),
                pl.BlockSpec((None, d, tn), lambda c, j, ce, cr, nu: (ce[cc(c, nu)], 0, jj(c, j, nu))),
                pl.BlockSpec((None, d, tn), lambda c, j, ce, cr, nu: (ce[cc(c, nu)], 0, nj + jj(c, j, nu))),
                pl.BlockSpec((None, 1, tn), lambda c, j, ce, cr, nu: (ce[cc(c, nu)], 0, jj(c, j, nu))),
                pl.BlockSpec((None, 1, tn), lambda c, j, ce, cr, nu: (ce[cc(c, nu)], 0, nj + jj(c, j, nu))),
            ] + [any_spec] * (2 * n_copy),
            out_specs=[pl.BlockSpec((cap, tn), lambda c, j, ce, cr, nu: (cc(c, nu), jj(c, j, nu)))] +
                      [any_spec] * n_copy,
            scratch_shapes=[pltpu.SemaphoreType.DMA((max(2 * n_copy, 1),))]),
        out_shape=[jax.ShapeDtypeStruct((n_chunks * cap, f), _BF)] + state_shapes,
        compiler_params=_params(("arbitrary", "arbitrary")),
        name="expert_up",
    )(c_exp, c_rows, n_used, xs, w_gate_up, w_gate_up, b3, b3, *caches, *new_rows)
    return res[0], list(res[1:])


def _expert_down_body(row_tile, c_exp_ref, c_rows_ref, n_used_ref, a_ref, w_ref, b_ref, o_ref):
    c = pl.program_id(0)
    cap = a_ref.shape[0]

    @pl.when(c < n_used_ref[0])
    def _():
        def tile(r0, rows):
            o_ref[pl.ds(r0, rows), :] = _wdot(a_ref[pl.ds(r0, rows), :], w_ref[...]) + b_ref[...]

        _chunk_rows_loop(c_rows_ref[c], cap, row_tile, tile, o_ref)


def _expert_down(act, w_down, b_down, plan, cap, row_tile, n_chunks):
    c_exp, c_rows, n_used = plan
    n_exp, f, d = w_down.shape
    tn = _pick(d, 512, LANES)
    nj = d // tn
    cc, jj = _chunk_maps(nj)
    return pl.pallas_call(
        functools.partial(_expert_down_body, row_tile),
        grid_spec=pltpu.PrefetchScalarGridSpec(
            num_scalar_prefetch=3,
            grid=(n_chunks, nj),
            in_specs=[
                pl.BlockSpec((cap, f), lambda c, j, ce, cr, nu: (cc(c, nu), 0)),
                pl.BlockSpec((None, f, tn), lambda c, j, ce, cr, nu: (ce[cc(c, nu)], 0, jj(c, j, nu))),
                pl.BlockSpec((None, 1, tn), lambda c, j, ce, cr, nu: (ce[cc(c, nu)], 0, jj(c, j, nu))),
            ],
            out_specs=pl.BlockSpec((cap, tn), lambda c, j, ce, cr, nu: (cc(c, nu), jj(c, j, nu)))),
        out_shape=jax.ShapeDtypeStruct((n_chunks * cap, d), _F32),
        compiler_params=_params(("arbitrary", "arbitrary")),
        name="expert_down",
    )(c_exp, c_rows, n_used, act, w_down, b_down.reshape(n_exp, 1, d))


def _combine_body(dest_ref, gate_ref, y_ref, o_ref, buf_ref, sem):
    tc = o_ref.shape[0]

    def row_copy(t, k):
        return pltpu.make_async_copy(y_ref.at[pl.ds(dest_ref[0, t * TOP_K + k], 1)],
                                     buf_ref.at[k, pl.ds(t, 1)], sem)

    def issue(t, _):
        for k in range(TOP_K):
            row_copy(t, k).start()
        return 0

    lax.fori_loop(0, tc, issue, 0, unroll=DMA_ISSUE_UNROLL // TOP_K)
    for k in range(TOP_K):
        pltpu.make_async_copy(y_ref.at[pl.ds(0, tc)], buf_ref.at[k], sem).wait()
    gate = gate_ref[...]
    acc = buf_ref[0] * gate[:, 0:1]
    for k in range(1, TOP_K):
        acc = acc + buf_ref[k] * gate[:, k:k + 1]
    o_ref[...] = acc


def _combine(y, dest, gate):
    m = gate.shape[0]
    d = y.shape[1]
    tc = _pick(m, 128, SUBLANES)
    n_tiles = m // tc
    return pl.pallas_call(
        _combine_body,
        grid=(n_tiles,),
        in_specs=[pl.BlockSpec((None, 1, tc * TOP_K), lambda i: (i, 0, 0), memory_space=pltpu.SMEM),
                  pl.BlockSpec((tc, LANES), lambda i: (i, 0)),
                  pl.BlockSpec(memory_space=pl.ANY)],
        out_specs=pl.BlockSpec((tc, d), lambda i: (i, 0)),
        out_shape=jax.ShapeDtypeStruct((m, d), _F32),
        scratch_shapes=[pltpu.VMEM((TOP_K, tc, d), _F32), pltpu.SemaphoreType.DMA(())],
        compiler_params=_params(("arbitrary",)),
        name="moe_combine",
    )(dest.reshape(n_tiles, 1, tc * TOP_K), gate, y)


def _moe(h2, h2p, w_router, b_router, w_gate_up, b_gate_up, w_down, b_down, layer, caches, new_rows, keeps):
    m = h2.shape[0]
    n_exp = w_router.shape[1]
    idx, gate, rank, counts = _router(h2, w_router, b_router)
    cap, row_tile, n_chunks = _moe_layout(m, n_exp)
    dest, plan, gather = _moe_plan(idx[:, :TOP_K], rank[:, :TOP_K], counts[0], cap, row_tile, n_chunks)
    xs = _dispatch(h2p, gather, cap, row_tile, n_chunks)
    act, states = _expert_up(xs, w_gate_up, b_gate_up, plan, cap, row_tile, n_chunks, layer, caches, new_rows, keeps)
    y = _expert_down(act, w_down, b_down, plan, cap, row_tile, n_chunks)
    return _combine(y, dest, gate), states


def kernel(x_prompt, x_sample, c_prompt, c_sample, cache_k_w128, cache_v_w128, cache_k_w512, cache_v_w512,
           cache_k_w2048, cache_v_w2048, state_conv, w_ada, b_ada, g_pre_mix, g_post_mix, g_pre_ffn, g_post_ffn,
           w_in, w_proj_a, w_dw, b_dw, g_conv_ln, b_conv_ln, w_pw2, b_pw2, w_o, w_router, b_router,
           w_gate_up, b_gate_up, w_down, b_down):
    B, S, D = x_prompt.shape
    DB, T, _ = x_sample.shape
    depth = w_ada.shape[0]
    C = w_dw.shape[2]
    H = HEADS_PER_GROUP
    bs, dbt = B * S, DB * T
    assert S % dbt == 0 and dbt % BF16_ROWS == 0 and S % ATTN_BLOCK == 0 and S >= CONV_HALO
    dims = (B, S, DB, T, D)
    k_caches = (cache_k_w128, cache_k_w512, cache_k_w2048)
    v_caches = (cache_v_w128, cache_v_w512, cache_v_w2048)
    glu_off = 3 * ATTN_WIDTH
    gate_off = glu_off + 2 * C
    tables = _rope_tables(B, S, DB, T, PAST_LEN)
    n_seq = B + DB
    c_all = jnp.concatenate([c_prompt, c_sample], axis=0)
    c_pad = jnp.pad(c_all, ((0, -n_seq % SUBLANES), (0, 0)))
    vec = lambda v: v.reshape(1, -1)

    x = (x_prompt.reshape(bs, D), x_sample.reshape(dbt, D))
    prompt_states, sample_states = [], []
    for l in range(depth):
        mod = _adaln(c_pad, w_ada[l], b_ada[l])
        (h,) = _tokenwise(_pre_mix_body, [x], [0, 1], [vec(g_pre_mix[l])], [(_BF, D, False)], mod, dims)
        proj = _in_proj(h, w_in[l], tables)

        attn_p = _attn_prompt(proj, B, S)
        qkv_s = proj[bs:, :3 * ATTN_WIDTH].reshape(DB, T, 9 * H, HEAD_DIM)
        attn_s = _attn_sample(qkv_s, [c[l] for c in k_caches], [c[l] for c in v_caches], DB, T)
        attn = jnp.concatenate([attn_p, attn_s.reshape(dbt, ATTN_OUT).astype(_BF)], axis=0)

        act_p, tail_p = _conv_prompt(proj, w_dw[l], b_dw[l], g_conv_ln[l], b_conv_ln[l], B, S, glu_off)
        glu_s = proj[bs:, glu_off:gate_off].reshape(DB, T, 2 * C)
        act_s, u_s = _conv_sample(glu_s, state_conv[l], w_dw[l], b_dw[l], g_conv_ln[l], b_conv_ln[l])
        act = jnp.concatenate([act_p, act_s.reshape(dbt, C).astype(_BF)], axis=0)

        z = _branch_merge(attn, w_proj_a[l], act, w_pw2[l], b_pw2[l], proj, gate_off)
        y = _matmul(z, w_o[l], _F32, "out_proj")
        x1, h2, h2p = _tokenwise(_post_mix_body, [x, y], [2, 3, 4], [vec(g_post_mix[l]), vec(g_pre_ffn[l])],
                                 [(_F32, D, False), (_F32, D, False), (jnp.uint32, D // 2, False)], mod, dims)
        st_p, caches, new_rows, keeps = [], [], [], []
        for g, (win, _) in enumerate(DILATED_GROUPS):
            for part, group_caches in ((1, k_caches), (2, v_caches)):
                lo = (part * N_GROUPS + g) * ATTN_OUT
                keep = min(win, S)
                cols = lax.slice(proj, (0, lo), (bs, lo + ATTN_OUT)).reshape(B, S, H, HEAD_DIM)
                st_p.append(cols[:, S - keep:])
                caches.append(group_caches[g])
                new_rows.append(proj[bs:, lo:lo + ATTN_OUT].reshape(DB, T, H, HEAD_DIM))
                keeps.append(min(win, group_caches[g].shape[2] + T))
        f, st_s = _moe(h2, h2p, w_router[l], b_router[l], w_gate_up[l], b_gate_up[l], w_down[l], b_down[l],
                       l, caches, new_rows, keeps)
        (x,) = _tokenwise(_post_ffn_body, [x1, f], [5], [vec(g_post_ffn[l])], [(_F32, D, True)], mod, dims)
        st_p.append(tail_p[:, CONV_HALO - (CONV_WIDTH - 1):])
        st_s.append(jnp.concatenate([state_conv[l], u_s], axis=1)[:, T:])
        prompt_states.append(st_p)
        sample_states.append(st_s)

    outs_p = [jnp.stack(s, axis=0) for s in zip(*prompt_states)]
    outs_s = [jnp.stack(s, axis=0) for s in zip(*sample_states)]
    return (x[0].reshape(B, S, D), x[1].reshape(DB, T, D), *outs_p, *outs_s)
```

```python
import functools

import jax
import jax.numpy as jnp
from jax import lax
from jax.experimental import pallas as pl
from jax.experimental.pallas import tpu as pltpu

HEAD_DIM = 128
HEADS_PER_GROUP = 8
DILATED_GROUPS = ((128, 1), (512, 4), (2048, 16))
N_GROUPS = len(DILATED_GROUPS)
ATTN_WIDTH = N_GROUPS * HEADS_PER_GROUP * HEAD_DIM
ATTN_OUT = HEADS_PER_GROUP * HEAD_DIM
ATTN_SCALE = HEAD_DIM ** -0.5
ROT_DIM = HEAD_DIM // 4
ROPE_THETA = 500000.0
CONV_WIDTH = 31
PAST_LEN = 8192
TOP_K = 4
SWIGLU_LIMIT = 7.0
SWIGLU_ALPHA = 1.702
RMS_EPS = 1e-6
LN_EPS = 1e-5

LANES = 128
SUBLANES = 8
BF16_ROWS = 16
ATTN_BLOCK = 128
CONV_HALO = 32
MOE_ROW_TILE = 256
DMA_ISSUE_UNROLL = 8
VMEM_LIMIT = 56 * 1024 * 1024

_BF = jnp.bfloat16
_F32 = jnp.float32


def _pick(n, target, align):
    best = None
    for t in range(align, min(n, target) + 1, align):
        if n % t == 0:
            best = t
    return best if best is not None else n


def _params(sem):
    return pltpu.CompilerParams(dimension_semantics=sem, vmem_limit_bytes=VMEM_LIMIT)


def _rms(x, g):
    return x * lax.rsqrt(jnp.mean(x * x, axis=-1, keepdims=True) + RMS_EPS) * g


def _adaln_body(c_ref, w_ref, b_ref, o_ref):
    c = c_ref[...]
    a = (c * jax.nn.sigmoid(c)).astype(_BF)
    o_ref[...] = jnp.dot(a, w_ref[...].astype(_BF), preferred_element_type=_F32) + b_ref[...]


def _adaln(c_pad, w, b):
    rows, d = c_pad.shape
    n = w.shape[1]
    tn = _pick(n, 512, LANES)
    return pl.pallas_call(
        _adaln_body,
        grid=(n // tn,),
        in_specs=[pl.BlockSpec((rows, d), lambda j: (0, 0)),
                  pl.BlockSpec((d, tn), lambda j: (0, j)),
                  pl.BlockSpec((1, tn), lambda j: (0, j))],
        out_specs=pl.BlockSpec((rows, tn), lambda j: (0, j)),
        out_shape=jax.ShapeDtypeStruct((rows, n), _F32),
        compiler_params=_params(("parallel",)),
        name="adaln",
    )(c_pad, w, b.reshape(1, n))


def _tokenwise(body, rows_in, mods, gains, outs, mod, dims):
    B, S, DB, T, D = dims
    bs, dbt = B * S, DB * T
    tr = dbt
    n_p = bs // tr
    per_seq = S // tr
    last_p = lambda i: jnp.minimum(i, n_p - 1)

    in_specs, args, pick_p, pick_s = [], [], [], []

    def add(spec, arr):
        in_specs.append(spec)
        args.append(arr)
        return len(args) - 1

    for r in rows_in:
        if isinstance(r, tuple):
            pick_p.append(add(pl.BlockSpec((tr, r[0].shape[1]), lambda i: (last_p(i), 0)), r[0]))
            pick_s.append(add(pl.BlockSpec((tr, r[1].shape[1]), lambda i: (0, 0)), r[1]))
        else:
            k = add(pl.BlockSpec((tr, r.shape[1]), lambda i: (i, 0)), r)
            pick_p.append(k)
            pick_s.append(k)
    mod_p = mod[:B].reshape(B, 1, mod.shape[1])
    mod_s = jnp.repeat(mod[B:B + DB], T, axis=0)
    for w in mods:
        pick_p.append(add(pl.BlockSpec((None, 1, D), lambda i, w=w: (last_p(i) // per_seq, 0, w)), mod_p))
        pick_s.append(add(pl.BlockSpec((tr, D), lambda i, w=w: (0, w)), mod_s))
    for g in gains:
        k = add(pl.BlockSpec((1, D), lambda i: (0, 0)), g)
        pick_p.append(k)
        pick_s.append(k)
    n_in = len(args)
    out_specs, out_shape, out_p, out_s = [], [], [], []
    for dt, width, split in outs:
        if split:
            out_specs.append(pl.BlockSpec((tr, width), lambda i: (last_p(i), 0)))
            out_shape.append(jax.ShapeDtypeStruct((bs, width), dt))
            out_p.append(len(out_specs) - 1)
            out_specs.append(pl.BlockSpec((tr, width), lambda i: (0, 0)))
            out_shape.append(jax.ShapeDtypeStruct((dbt, width), dt))
            out_s.append(len(out_specs) - 1)
        else:
            out_specs.append(pl.BlockSpec((tr, width), lambda i: (i, 0)))
            out_shape.append(jax.ShapeDtypeStruct((bs + dbt, width), dt))
            out_p.append(len(out_specs) - 1)
            out_s.append(len(out_specs) - 1)

    def both(*refs):
        i = pl.program_id(0)

        @pl.when(i < n_p)
        def _():
            body(*[refs[k] for k in pick_p], *[refs[n_in + k] for k in out_p])

        @pl.when(i == n_p)
        def _():
            body(*[refs[k] for k in pick_s], *[refs[n_in + k] for k in out_s])

    res = pl.pallas_call(
        both, grid=(n_p + 1,), in_specs=in_specs, out_specs=out_specs, out_shape=out_shape,
        compiler_params=_params(("arbitrary",)), name=body.__name__.strip("_"),
    )(*args)
    return [(res[out_p[k]], res[out_s[k]]) if outs[k][2] else res[out_p[k]] for k in range(len(outs))]


def _pre_mix_body(x_ref, shift_ref, scale_ref, g_ref, h_ref):
    h = _rms(x_ref[...], g_ref[...]) * (1.0 + scale_ref[...]) + shift_ref[...]
    h_ref[...] = h.astype(_BF)


def _pack_bf16_pair(h):
    half = h.shape[1] // 2
    lo = pltpu.bitcast(h[:, :half].astype(_BF).astype(_F32), jnp.uint32) >> 16
    hi = pltpu.bitcast(h[:, half:].astype(_BF).astype(_F32), jnp.uint32) & jnp.uint32(0xFFFF0000)
    return hi | lo


def _post_mix_body(x_ref, y_ref, gate_ref, shift_ref, scale_ref, gpost_ref, gpre_ref, x1_ref, h2_ref, h2p_ref):
    x1 = x_ref[...] + gate_ref[...] * _rms(y_ref[...], gpost_ref[...])
    x1_ref[...] = x1
    h2 = _rms(x1, gpre_ref[...]) * (1.0 + scale_ref[...]) + shift_ref[...]
    h2_ref[...] = h2
    h2p_ref[...] = _pack_bf16_pair(h2)


def _post_ffn_body(x1_ref, f_ref, gate_ref, gpost_ref, o_ref):
    o_ref[...] = x1_ref[...] + gate_ref[...] * _rms(f_ref[...], gpost_ref[...])


def _mm_body(a_ref, w_ref, o_ref):
    o_ref[...] = _wdot(a_ref[...], w_ref[...]).astype(o_ref.dtype)


def _matmul(a, w, out_dtype, name):
    m, kd = a.shape
    n = w.shape[1]
    tm = _pick(m, 1664, BF16_ROWS)
    tn = _pick(n, 2 * LANES, LANES)
    return pl.pallas_call(
        _mm_body,
        grid=(m // tm, n // tn),
        in_specs=[pl.BlockSpec((tm, kd), lambda i, j: (i, 0)),
                  pl.BlockSpec((kd, tn), lambda i, j: (0, j))],
        out_specs=pl.BlockSpec((tm, tn), lambda i, j: (i, j)),
        out_shape=jax.ShapeDtypeStruct((m, n), out_dtype),
        compiler_params=_params(("parallel", "parallel")),
        name=name,
    )(a, w)


def _rope_tile(x, c, s_lo, s_hi):
    half = ROT_DIM // 2
    parts = []
    for hb in range(x.shape[1] // HEAD_DIM):
        xh = x[:, hb * HEAD_DIM:(hb + 1) * HEAD_DIM]
        up = pltpu.roll(xh, HEAD_DIM - half, 1)
        dn = pltpu.roll(xh, half, 1)
        parts.append(xh * c + up * s_lo + dn * s_hi)
    return jnp.concatenate(parts, axis=1) if len(parts) > 1 else parts[0]


def _in_proj_body(n_rope_tiles, a_ref, w_ref, c_ref, slo_ref, shi_ref, o_ref):
    j = pl.program_id(1)
    y = _wdot(a_ref[...], w_ref[...])

    @pl.when(j < n_rope_tiles)
    def _():
        o_ref[...] = _rope_tile(y, c_ref[...], slo_ref[...], shi_ref[...])

    @pl.when(j >= n_rope_tiles)
    def _():
        o_ref[...] = y


def _in_proj(h, w_in, tables):
    m, d = h.shape
    n = w_in.shape[1]
    tm = _pick(m, 1664, BF16_ROWS)
    tn = 2 * LANES if n % (2 * LANES) == 0 else LANES
    assert (2 * ATTN_WIDTH) % tn == 0
    tab_spec = pl.BlockSpec((tm, HEAD_DIM), lambda i, j: (i, 0))
    return pl.pallas_call(
        functools.partial(_in_proj_body, 2 * ATTN_WIDTH // tn),
        grid=(m // tm, n // tn),
        in_specs=[pl.BlockSpec((tm, d), lambda i, j: (i, 0)),
                  pl.BlockSpec((d, tn), lambda i, j: (0, j)),
                  tab_spec, tab_spec, tab_spec],
        out_specs=pl.BlockSpec((tm, tn), lambda i, j: (i, j)),
        out_shape=jax.ShapeDtypeStruct((m, n), _F32),
        compiler_params=_params(("parallel", "parallel")),
        name="in_proj",
    )(h, w_in, *tables)


def _rope_tables(B, S, DB, T, past_len):
    half = ROT_DIM // 2
    pos = jnp.concatenate([jnp.tile(jnp.arange(S, dtype=jnp.int32), B),
                           jnp.tile(past_len + jnp.arange(T, dtype=jnp.int32), DB)])
    inv_freq = ROPE_THETA ** (-jnp.arange(half, dtype=_F32) / half)
    ang = pos.astype(_F32)[:, None] * inv_freq[None, :]
    cos, sin = jnp.cos(ang), jnp.sin(ang)
    rows = pos.shape[0]
    pad = HEAD_DIM - ROT_DIM
    c = jnp.concatenate([cos, cos, jnp.ones((rows, pad), _F32)], axis=1)
    s_lo = jnp.concatenate([-sin, jnp.zeros((rows, half + pad), _F32)], axis=1)
    s_hi = jnp.concatenate([jnp.zeros((rows, half), _F32), sin, jnp.zeros((rows, pad), _F32)], axis=1)
    return c, s_lo, s_hi


def _branch_merge_body(a1_ref, w1_ref, a2_ref, w2_ref, b2_ref, ga_ref, gb_ref, o_ref):
    ya = jnp.dot(a1_ref[...], w1_ref[...].astype(_BF), preferred_element_type=_F32)
    yb = jnp.dot(a2_ref[...], w2_ref[...].astype(_BF), preferred_element_type=_F32) + b2_ref[...]
    z = jax.nn.sigmoid(ga_ref[...]) * ya + jax.nn.sigmoid(gb_ref[...]) * yb
    o_ref[...] = z.astype(o_ref.dtype)


def _branch_merge(attn, w_proj_a, act, w_pw2, b_pw2, proj, gate_off):
    m, d = attn.shape[0], w_proj_a.shape[1]
    tm = _pick(m, 832, BF16_ROWS)
    tn = LANES
    for t in (256, 512):
        if d % t == 0 and gate_off % t == 0:
            tn = t
    ga_blk, gb_blk = gate_off // tn, (gate_off + d) // tn
    return pl.pallas_call(
        _branch_merge_body,
        grid=(m // tm, d // tn),
        in_specs=[pl.BlockSpec((tm, attn.shape[1]), lambda i, j: (i, 0)),
                  pl.BlockSpec((w_proj_a.shape[0], tn), lambda i, j: (0, j)),
                  pl.BlockSpec((tm, act.shape[1]), lambda i, j: (i, 0)),
                  pl.BlockSpec((w_pw2.shape[0], tn), lambda i, j: (0, j)),
                  pl.BlockSpec((1, tn), lambda i, j: (0, j)),
                  pl.BlockSpec((tm, tn), lambda i, j: (i, ga_blk + j)),
                  pl.BlockSpec((tm, tn), lambda i, j: (i, gb_blk + j))],
        out_specs=pl.BlockSpec((tm, tn), lambda i, j: (i, j)),
        out_shape=jax.ShapeDtypeStruct((m, d), _BF),
        compiler_params=_params(("parallel", "parallel")),
        name="branch_merge",
    )(attn, w_proj_a, act, w_pw2, b_pw2.reshape(1, d), proj, proj)


def _attn_plan(S):
    blk = ATTN_BLOCK
    chunk = _pick(S, 512, blk)
    plan, off = [], 0
    for win, _ in DILATED_GROUPS:
        width = (win // blk + 1) * blk
        if width <= min(S, 1024):
            plan.append(("window", width, off))
            off += width
        else:
            plan.append(("chunks", chunk))
    return plan, off, chunk


def _attn_scores(q, k, row0, col0, win, dil):
    s = lax.dot_general(q, k, (((1,), (1,)), ((), ())), preferred_element_type=_F32) * ATTN_SCALE
    dist = (row0 - col0) + lax.broadcasted_iota(jnp.int32, s.shape, 0) - lax.broadcasted_iota(jnp.int32, s.shape, 1)
    valid = (dist >= 0) & (dist <= win)
    if dil > 1:
        valid = valid & ((dist & (dil - 1)) == 0)
    return jnp.where(valid, s, -jnp.inf)


def _attn_prompt_body(plan, *refs):
    q_refs, k_refs, v_refs, o_ref, sw_ref, sc_ref = refs[0:3], refs[3:6], refs[6:9], refs[9], refs[10], refs[11]
    blk = ATTN_BLOCK
    S = o_ref.shape[0]
    n_blk = S // blk
    rowmax = lambda s: jnp.max(s, axis=1, keepdims=True)

    def q_block(i, _):
        r0 = pl.multiple_of(i * blk, blk)
        qs = [q_refs[g][pl.ds(r0, blk), :].astype(_BF) for g in range(N_GROUPS)]
        starts = {}
        m = jnp.full((blk, 1), -jnp.inf, _F32)
        for g, (win, dil) in enumerate(DILATED_GROUPS):
            if plan[g][0] == "window":
                _, width, off = plan[g]
                c0 = pl.multiple_of(jnp.clip(i - win // blk, 0, (S - width) // blk) * blk, blk)
                starts[g] = c0
                s = _attn_scores(qs[g], k_refs[g][pl.ds(c0, width), :].astype(_BF), r0, c0, win, dil)
                sw_ref[:, off:off + width] = s
                m = jnp.maximum(m, rowmax(s))
            else:
                ch = plan[g][1]
                first = jnp.maximum(r0 - win, 0) // ch

                def score_chunk(c, m, g=g, win=win, dil=dil, ch=ch):
                    c0 = pl.multiple_of(c * ch, ch)
                    s = _attn_scores(qs[g], k_refs[g][pl.ds(c0, ch), :].astype(_BF), r0, c0, win, dil)
                    sc_ref[c] = s
                    return jnp.maximum(m, rowmax(s))

                m = lax.fori_loop(first, r0 // ch + 1, score_chunk, m)
        l = jnp.zeros((blk, 1), _F32)
        acc = jnp.zeros((blk, HEAD_DIM), _F32)
        for g, (win, dil) in enumerate(DILATED_GROUPS):
            if plan[g][0] == "window":
                _, width, off = plan[g]
                p = jnp.exp(sw_ref[:, off:off + width] - m)
                l = l + jnp.sum(p, axis=1, keepdims=True)
                v = v_refs[g][pl.ds(starts[g], width), :].astype(_BF)
                acc = acc + jnp.dot(p.astype(_BF), v, preferred_element_type=_F32)
            else:
                ch = plan[g][1]
                first = jnp.maximum(r0 - win, 0) // ch

                def value_chunk(c, carry, g=g, ch=ch):
                    l, acc = carry
                    c0 = pl.multiple_of(c * ch, ch)
                    p = jnp.exp(sc_ref[c] - m)
                    v = v_refs[g][pl.ds(c0, ch), :].astype(_BF)
                    return (l + jnp.sum(p, axis=1, keepdims=True),
                            acc + jnp.dot(p.astype(_BF), v, preferred_element_type=_F32))

                l, acc = lax.fori_loop(first, r0 // ch + 1, value_chunk, (l, acc))
        o_ref[pl.ds(r0, blk), :] = (acc / l).astype(o_ref.dtype)
        return 0

    lax.fori_loop(0, n_blk, q_block, 0)


def _attn_prompt(proj, B, S):
    H = HEADS_PER_GROUP
    assert all(dil & (dil - 1) == 0 for _, dil in DILATED_GROUPS) and DILATED_GROUPS[0][1] == 1
    plan, win_width, chunk = _attn_plan(S)
    specs = []
    for part in range(3):
        for g in range(N_GROUPS):
            off = part * N_GROUPS * H + g * H
            specs.append(pl.BlockSpec((S, HEAD_DIM), lambda b, h, off=off: (b, off + h)))
    return pl.pallas_call(
        functools.partial(_attn_prompt_body, plan),
        grid=(B, H),
        in_specs=specs,
        out_specs=pl.BlockSpec((S, HEAD_DIM), lambda b, h: (b, h)),
        out_shape=jax.ShapeDtypeStruct((B * S, ATTN_OUT), _BF),
        scratch_shapes=[pltpu.VMEM((ATTN_BLOCK, max(win_width, LANES)), _F32),
                        pltpu.VMEM((S // chunk, ATTN_BLOCK, chunk), _F32)],
        compiler_params=_params(("parallel", "parallel")),
        name="attn_prompt",
    )(*([proj] * 9))


def _attn_sample_body(T, qkv_ref, *refs):
    kc_refs, vc_refs, o_ref = refs[0:3], refs[3:6], refs[6]
    H = HEADS_PER_GROUP
    qkv = qkv_ref[...]
    for t in range(T):
        scores, values = [], []
        for g, (win, dil) in enumerate(DILATED_GROUPS):
            q = qkv[t, g * H:(g + 1) * H, :]
            k_new = qkv[:, (N_GROUPS + g) * H:(N_GROUPS + g + 1) * H, :]
            v_new = qkv[:, (2 * N_GROUPS + g) * H:(2 * N_GROUPS + g + 1) * H, :]
            rows = win // dil
            if dil == 1:
                k_old, v_old = kc_refs[g][...], vc_refs[g][...]
                idx_old = lax.broadcasted_iota(jnp.int32, (rows, H, 1), 0)
            else:
                k_old, v_old = kc_refs[g][:, t], vc_refs[g][:, t]
                idx_old = t + dil * lax.broadcasted_iota(jnp.int32, (rows, H, 1), 0)
            idx_new = win + lax.broadcasted_iota(jnp.int32, (T, H, 1), 0)
            for kk, vv, idx in ((k_old, v_old, idx_old), (k_new, v_new, idx_new)):
                dist = (win + t) - idx
                valid = (dist >= 0) & (dist <= win) & (lax.rem(dist, dil) == 0)
                s = jnp.sum(kk * q[None], axis=-1, keepdims=True) * ATTN_SCALE
                scores.append(jnp.where(valid, s, -jnp.inf))
                values.append(vv)
        m = functools.reduce(jnp.maximum, [jnp.max(s, axis=0, keepdims=True) for s in scores])
        den = jnp.zeros((1, H, 1), _F32)
        num = jnp.zeros((1, H, HEAD_DIM), _F32)
        for s, vv in zip(scores, values):
            p = jnp.exp(s - m)
            den = den + jnp.sum(p, axis=0, keepdims=True)
            num = num + jnp.sum(p * vv, axis=0, keepdims=True)
        o_ref[pl.ds(t, 1)] = num / den


def _attn_sample(qkv_s, k_caches, v_caches, DB, T):
    H = HEADS_PER_GROUP
    args, specs = [qkv_s], [pl.BlockSpec((None, T, 9 * H, HEAD_DIM), lambda b: (b, 0, 0, 0))]
    for caches in (k_caches, v_caches):
        for g, (win, dil) in enumerate(DILATED_GROUPS):
            c = caches[g]
            assert c.shape[1] == win and win % dil == 0 and (dil == 1 or T <= dil)
            if dil == 1:
                args.append(c)
                specs.append(pl.BlockSpec((None, win, H, HEAD_DIM), lambda b: (b, 0, 0, 0)))
            else:
                args.append(c.reshape(DB, win // dil, dil, H, HEAD_DIM))
                specs.append(pl.BlockSpec((None, win // dil, T, H, HEAD_DIM), lambda b: (b, 0, 0, 0, 0)))
    return pl.pallas_call(
        functools.partial(_attn_sample_body, T),
        grid=(DB,),
        in_specs=specs,
        out_specs=pl.BlockSpec((None, T, H, HEAD_DIM), lambda b: (b, 0, 0, 0)),
        out_shape=jax.ShapeDtypeStruct((DB, T, H, HEAD_DIM), _F32),
        compiler_params=_params(("parallel",)),
        name="attn_sample",
    )(*args)


def _ln_silu(conv, g, b):
    xc = conv - jnp.mean(conv, axis=-1, keepdims=True)
    y = xc * lax.rsqrt(jnp.mean(xc * xc, axis=-1, keepdims=True) + LN_EPS) * g + b
    return y * jax.nn.sigmoid(y)


def _conv_prompt_body(n_piece, *refs):
    ua_refs, ub_refs = refs[:n_piece], refs[n_piece:2 * n_piece]
    wdw_ref, bdw_ref, g_ref, b_ref, act_ref, tail_ref, ext_ref, conv_ref = refs[2 * n_piece:]
    i = pl.program_id(1)
    tt = act_ref.shape[0]
    cw = ua_refs[0].shape[1]
    halo = CONV_HALO

    @pl.when(i == 0)
    def _():
        ext_ref[0:halo, :] = jnp.zeros((halo, ext_ref.shape[1]), _F32)

    for p in range(n_piece):
        ext_ref[halo:halo + tt, p * cw:(p + 1) * cw] = ua_refs[p][...] * jax.nn.sigmoid(ub_refs[p][...])

    rc = _pick(tt, 32, SUBLANES)
    cc = _pick(ext_ref.shape[1], 256, LANES)
    off = halo - (CONV_WIDTH - 1)
    win_rows = rc + halo

    def row_chunk(r, _):
        r0 = pl.multiple_of(r * rc, rc)
        for c0 in range(0, ext_ref.shape[1], cc):
            win = ext_ref[pl.ds(r0, win_rows), c0:c0 + cc]
            acc = jnp.broadcast_to(bdw_ref[:, c0:c0 + cc], (rc, cc))
            for res in range(SUBLANES):
                sh = win if res == 0 else pltpu.roll(win, win_rows - res, 0)
                for a0 in range(0, halo + 1, SUBLANES):
                    w = a0 + res - off
                    if 0 <= w < CONV_WIDTH:
                        acc = acc + sh[a0:a0 + rc] * wdw_ref[w:w + 1, c0:c0 + cc]
            conv_ref[pl.ds(r0, rc), c0:c0 + cc] = acc
        return 0

    lax.fori_loop(0, tt // rc, row_chunk, 0)
    act_ref[...] = _ln_silu(conv_ref[...], g_ref[...], b_ref[...]).astype(act_ref.dtype)
    tail = ext_ref[tt:tt + halo, :]
    tail_ref[...] = tail
    ext_ref[0:halo, :] = tail


def _conv_prompt(proj, w_dw, b_dw, g_ln, b_ln, B, S, glu_off):
    C = w_dw.shape[1]
    tt = _pick(S, 256, BF16_ROWS)
    nt = S // tt
    cw = LANES
    for t in range(LANES, 1024 + 1, LANES):
        if C % t == 0 and glu_off % t == 0:
            cw = t
    n_piece = C // cw
    specs = []
    for half in range(2):
        for p in range(n_piece):
            blk = (glu_off + half * C) // cw + p
            specs.append(pl.BlockSpec((tt, cw), lambda b, i, blk=blk: (b * nt + i, blk)))
    vec = lambda rows: pl.BlockSpec((rows, C), lambda b, i: (0, 0))
    act, tail = pl.pallas_call(
        functools.partial(_conv_prompt_body, n_piece),
        grid=(B, nt),
        in_specs=specs + [vec(CONV_WIDTH), vec(1), vec(1), vec(1)],
        out_specs=[pl.BlockSpec((tt, C), lambda b, i: (b * nt + i, 0)),
                   pl.BlockSpec((None, CONV_HALO, C), lambda b, i: (b, 0, 0))],
        out_shape=[jax.ShapeDtypeStruct((B * S, C), _BF), jax.ShapeDtypeStruct((B, CONV_HALO, C), _F32)],
        scratch_shapes=[pltpu.VMEM((CONV_HALO + tt, C), _F32), pltpu.VMEM((tt, C), _F32)],
        compiler_params=_params(("arbitrary", "arbitrary")),
        name="conv_prompt",
    )(*([proj] * (2 * n_piece)), w_dw, b_dw.reshape(1, C), g_ln.reshape(1, C), b_ln.reshape(1, C))
    return act, tail


def _conv_sample_body(T, glu_ref, state_ref, wst_ref, wu_ref, bdw_ref, g_ref, b_ref, act_ref, u_ref):
    C = u_ref.shape[2]
    glu = glu_ref[...]
    u = glu[:, :, :C] * jax.nn.sigmoid(glu[:, :, C:])
    u_ref[...] = u
    state = state_ref[...]
    for t in range(T):
        conv = jnp.sum(state * wst_ref[t][None], axis=1) + jnp.sum(u * wu_ref[t][None], axis=1) + bdw_ref[...]
        act_ref[:, t, :] = _ln_silu(conv, g_ref[...], b_ref[...])


def _conv_sample(glu_s, state, w_dw, b_dw, g_ln, b_ln):
    DB, T, _ = glu_s.shape
    C = w_dw.shape[1]
    n_st = CONV_WIDTH - 1
    r = jnp.arange(n_st)[None, :] - jnp.arange(T)[:, None]
    w_state = jnp.where((r >= 0)[..., None], w_dw[jnp.clip(r, 0, CONV_WIDTH - 1)], 0.0)
    ru = n_st - jnp.arange(T)[:, None] + jnp.arange(T)[None, :]
    w_new = jnp.where((ru <= n_st)[..., None], w_dw[jnp.clip(ru, 0, CONV_WIDTH - 1)], 0.0)
    full = lambda shape: pl.BlockSpec(shape, lambda i: (0,) * len(shape))
    act, u = pl.pallas_call(
        functools.partial(_conv_sample_body, T),
        grid=(1,),
        in_specs=[full(glu_s.shape), full(state.shape), full(w_state.shape), full(w_new.shape),
                  full((1, C)), full((1, C)), full((1, C))],
        out_specs=[full((DB, T, C)), full((DB, T, C))],
        out_shape=[jax.ShapeDtypeStruct((DB, T, C), _F32), jax.ShapeDtypeStruct((DB, T, C), _F32)],
        compiler_params=_params(("arbitrary",)),
        name="conv_sample",
    )(glu_s, state, w_state, w_new, b_dw.reshape(1, C), g_ln.reshape(1, C), b_ln.reshape(1, C))
    return act, u


def _router_body(n_exp, h_ref, w_ref, b_ref, idx_ref, gate_ref, rank_ref, cnt_ref, carry_ref):
    i = pl.program_id(0)
    tm = h_ref.shape[0]

    @pl.when(i == 0)
    def _():
        carry_ref[...] = jnp.zeros_like(carry_ref)

    logits = jnp.dot(h_ref[...], w_ref[...], preferred_element_type=_F32,
                     precision=lax.Precision.HIGHEST) + b_ref[...]
    lane = lax.broadcasted_iota(jnp.int32, (tm, n_exp), 1)
    lane_f = lane.astype(_F32)
    out_lane = lax.broadcasted_iota(jnp.int32, (tm, LANES), 1)
    work = logits
    vals, idxs = [], []
    for _ in range(TOP_K):
        v = jnp.max(work, axis=1, keepdims=True)
        ix = jnp.min(jnp.where(work == v, lane_f, float(n_exp)), axis=1, keepdims=True).astype(jnp.int32)
        vals.append(v)
        idxs.append(ix)
        work = jnp.where(lane == ix, -jnp.inf, work)
    exps = [jnp.exp(v - vals[0]) for v in vals]
    den = functools.reduce(jnp.add, exps)
    onehot = functools.reduce(jnp.add, [(lane == ix).astype(_F32) for ix in idxs])
    tri = (lax.broadcasted_iota(jnp.int32, (tm, tm), 1) < lax.broadcasted_iota(jnp.int32, (tm, tm), 0)).astype(_BF)
    before = jnp.dot(tri, onehot.astype(_BF), preferred_element_type=_F32) + carry_ref[...]
    idx_out = jnp.zeros((tm, LANES), jnp.int32)
    gate_out = jnp.zeros((tm, LANES), _F32)
    rank_out = jnp.zeros((tm, LANES), jnp.int32)
    for k in range(TOP_K):
        rk = jnp.sum(jnp.where(lane == idxs[k], before, 0.0), axis=1, keepdims=True).astype(jnp.int32)
        idx_out = jnp.where(out_lane == k, idxs[k], idx_out)
        gate_out = jnp.where(out_lane == k, exps[k] / den, gate_out)
        rank_out = jnp.where(out_lane == k, rk, rank_out)
    idx_ref[...] = idx_out
    gate_ref[...] = gate_out
    rank_ref[...] = rank_out
    carry_ref[...] += jnp.sum(onehot, axis=0, keepdims=True)
    cnt_ref[...] = carry_ref[...].astype(jnp.int32)


def _router(h2, w_router, b_router):
    m, d = h2.shape
    n_exp = w_router.shape[1]
    tm = _pick(m, 640, SUBLANES)
    row = lambda width: pl.BlockSpec((tm, width), lambda i: (i, 0))
    return pl.pallas_call(
        functools.partial(_router_body, n_exp),
        grid=(m // tm,),
        in_specs=[row(d), pl.BlockSpec((d, n_exp), lambda i: (0, 0)), pl.BlockSpec((1, n_exp), lambda i: (0, 0))],
        out_specs=[row(LANES), row(LANES), row(LANES), pl.BlockSpec((1, n_exp), lambda i: (0, 0))],
        out_shape=[jax.ShapeDtypeStruct((m, LANES), jnp.int32), jax.ShapeDtypeStruct((m, LANES), _F32),
                   jax.ShapeDtypeStruct((m, LANES), jnp.int32), jax.ShapeDtypeStruct((1, n_exp), jnp.int32)],
        scratch_shapes=[pltpu.VMEM((1, n_exp), _F32)],
        compiler_params=_params(("arbitrary",)),
        name="router",
    )(h2, w_router, b_router.reshape(1, n_exp))


def _moe_layout(m, n_exp):
    n_assign = m * TOP_K
    row_tile = MOE_ROW_TILE
    cap = -(-(n_assign * 3 // 2) // (n_exp * row_tile)) * row_tile
    n_chunks = n_exp + n_assign // cap
    return cap, row_tile, n_chunks


def _moe_plan(idx, rank, counts, cap, row_tile, n_chunks):
    n_exp = counts.shape[0]
    n_assign = idx.size
    i32 = jnp.int32
    chunks_per = (counts + cap - 1) // cap
    chunk_end = jnp.cumsum(chunks_per)
    chunk_start = chunk_end - chunks_per
    n_used = chunk_end[-1]
    dest = (chunk_start[idx] + rank // cap) * cap + rank % cap
    c = jnp.arange(n_chunks, dtype=i32)
    c_exp = jnp.minimum(jnp.searchsorted(chunk_end, c, side="right"), n_exp - 1).astype(i32)
    c_rows = jnp.clip(counts[c_exp] - (c - chunk_start[c_exp]) * cap, 0, cap)
    c_rows = jnp.where(c < n_used, c_rows, 0).astype(i32)

    order = jnp.argsort(idx.reshape(-1), stable=True).astype(i32)
    exp_start = jnp.cumsum(counts) - counts
    tiles_per = (c_rows + row_tile - 1) // row_tile
    tile_end = jnp.cumsum(tiles_per)
    tile_start = tile_end - tiles_per
    n_tiles_used = tile_end[-1]
    n_tiles = n_assign // row_tile + n_chunks
    t = jnp.arange(n_tiles, dtype=i32)
    t_c = jnp.minimum(jnp.searchsorted(tile_end, t, side="right"), n_chunks - 1).astype(i32)
    t_k = t - tile_start[t_c]
    blk = t_c * (cap // row_tile) + t_k
    blk = jnp.where(t < n_tiles_used, blk, blk[jnp.maximum(n_tiles_used - 1, 0)]).astype(i32)
    r = t_k[:, None] * row_tile + jnp.arange(row_tile, dtype=i32)[None, :]
    e = c_exp[t_c]
    pos = exp_start[e][:, None] + (t_c - chunk_start[e])[:, None] * cap + r
    row_ok = (t < n_tiles_used)[:, None] & (r < c_rows[t_c][:, None])
    tok = jnp.where(row_ok, order[jnp.clip(pos, 0, n_assign - 1)] // TOP_K, 0).astype(i32)
    gather = (blk, n_tiles_used.astype(i32).reshape(1), tok)
    return dest.astype(i32), (c_exp, c_rows, n_used.astype(i32).reshape(1)), gather


def _dispatch_body(blk_ref, n_tiles_ref, tok_ref, h_ref, o_ref, sem):
    t = pl.program_id(0)

    def row_copy(r):
        return pltpu.make_async_copy(h_ref.at[pl.ds(tok_ref[0, r], 1)], o_ref.at[pl.ds(r, 1)], sem)

    @pl.when(t < n_tiles_ref[0])
    def _():
        def issue(r, _):
            row_copy(r).start()
            return 0

        lax.fori_loop(0, o_ref.shape[0], issue, 0, unroll=DMA_ISSUE_UNROLL)
        pltpu.make_async_copy(h_ref.at[pl.ds(0, o_ref.shape[0])], o_ref, sem).wait()


def _dispatch(h2p, gather, cap, row_tile, n_chunks):
    blk, n_tiles_used, tok = gather
    half = h2p.shape[1]
    n_tiles = tok.shape[0]
    return pl.pallas_call(
        _dispatch_body,
        grid_spec=pltpu.PrefetchScalarGridSpec(
            num_scalar_prefetch=2,
            grid=(n_tiles,),
            in_specs=[pl.BlockSpec((None, 1, row_tile), lambda t, blk, nt: (t, 0, 0), memory_space=pltpu.SMEM),
                      pl.BlockSpec(memory_space=pl.ANY)],
            out_specs=pl.BlockSpec((row_tile, half), lambda t, blk, nt: (blk[t], 0)),
            scratch_shapes=[pltpu.SemaphoreType.DMA(())]),
        out_shape=jax.ShapeDtypeStruct((n_chunks * cap, half), jnp.uint32),
        compiler_params=_params(("arbitrary",)),
        name="moe_dispatch",
    )(blk, n_tiles_used, tok.reshape(n_tiles, 1, row_tile), h2p)


def _chunk_rows_loop(rows, cap, row_tile, tile, o_ref):
    unit = row_tile // 2
    big = 2 * row_tile
    n_unit = (rows + unit - 1) // unit
    n_big = n_unit // 4
    rem = n_unit % 4

    def full(t, _):
        tile(pl.multiple_of(t * big, big), big)
        return 0

    lax.fori_loop(0, n_big, full, 0)

    @pl.when(rem >= 2)
    def _():
        tile(pl.multiple_of(n_big * big, big), row_tile)

    @pl.when(rem % 2 == 1)
    def _():
        tile(pl.multiple_of(n_big * big + (rem // 2) * row_tile, row_tile), unit)

    def clear(t, _):
        r0 = pl.multiple_of(t * unit, unit)
        o_ref[pl.ds(r0, unit), :] = jnp.zeros((unit, o_ref.shape[1]), o_ref.dtype)
        return 0

    lax.fori_loop(n_unit, cap // unit, clear, 0)


def _unpack_bf16_pair(xp):
    lo = pltpu.bitcast(xp << 16, _F32).astype(_BF)
    hi = pltpu.bitcast(xp & jnp.uint32(0xFFFF0000), _F32).astype(_BF)
    return lo, hi


def _wdot(x, w):
    return lax.dot_general(x, w, (((1,), (0,)), ((), ())), preferred_element_type=_F32)


def _expert_up_body(row_tile, c_exp_ref, c_rows_ref, n_used_ref, x_ref, wg_ref, wu_ref, bg_ref, bu_ref, o_ref):
    c = pl.program_id(0)
    cap, half = x_ref.shape

    @pl.when(c < n_used_ref[0])
    def _():
        def tile(r0, rows):
            lo, hi = _unpack_bf16_pair(x_ref[pl.ds(r0, rows), :])
            g = _wdot(lo, wg_ref[:half, :]) + _wdot(hi, wg_ref[half:, :]) + bg_ref[...]
            u = _wdot(lo, wu_ref[:half, :]) + _wdot(hi, wu_ref[half:, :]) + bu_ref[...]
            g = jnp.minimum(g, SWIGLU_LIMIT)
            u = jnp.clip(u, -SWIGLU_LIMIT, SWIGLU_LIMIT)
            act = (u + 1.0) * (g * jax.nn.sigmoid(SWIGLU_ALPHA * g))
            o_ref[pl.ds(r0, rows), :] = act.astype(o_ref.dtype)

        _chunk_rows_loop(c_rows_ref[c], cap, row_tile, tile, o_ref)


def _chunk_maps(nj):
    def cc(c, nu):
        return jnp.minimum(c, nu[0] - 1)

    def jj(c, j, nu):
        return jnp.where(c < nu[0], j, nj - 1)

    return cc, jj


def _expert_up(xs, w_gate_up, b_gate_up, plan, cap, row_tile, n_chunks):
    c_exp, c_rows, n_used = plan
    n_exp, d, f2 = w_gate_up.shape
    f = f2 // 2
    tn = _pick(f, 256, LANES)
    nj = f // tn
    cc, jj = _chunk_maps(nj)
    b3 = b_gate_up.reshape(n_exp, 1, f2)
    return pl.pallas_call(
        functools.partial(_expert_up_body, row_tile),
        grid_spec=pltpu.PrefetchScalarGridSpec(
            num_scalar_prefetch=3,
            grid=(n_chunks, nj),
            in_specs=[
                pl.BlockSpec((cap, d // 2), lambda c, j, ce, cr, nu: (cc(c, nu), 0)),
                pl.BlockSpec((None, d, tn), lambda c, j, ce, cr, nu: (ce[cc(c, nu)], 0, jj(c, j, nu))),
                pl.BlockSpec((None, d, tn), lambda c, j, ce, cr, nu: (ce[cc(c, nu)], 0, nj + jj(c, j, nu))),
                pl.BlockSpec((None, 1, tn), lambda c, j, ce, cr, nu: (ce[cc(c, nu)], 0, jj(c, j, nu))),
                pl.BlockSpec((None, 1, tn), lambda c, j, ce, cr, nu: (ce[cc(c, nu)], 0, nj + jj(c, j, nu))),
            ],
            out_specs=pl.BlockSpec((cap, tn), lambda c, j, ce, cr, nu: (cc(c, nu), jj(c, j, nu)))),
        out_shape=jax.ShapeDtypeStruct((n_chunks * cap, f), _BF),
        compiler_params=_params(("arbitrary", "arbitrary")),
        name="expert_up",
    )(c_exp, c_rows, n_used, xs, w_gate_up, w_gate_up, b3, b3)


def _expert_down_body(row_tile, c_exp_ref, c_rows_ref, n_used_ref, a_ref, w_ref, b_ref, o_ref):
    c = pl.program_id(0)
    cap = a_ref.shape[0]

    @pl.when(c < n_used_ref[0])
    def _():
        def tile(r0, rows):
            o_ref[pl.ds(r0, rows), :] = _wdot(a_ref[pl.ds(r0, rows), :], w_ref[...]) + b_ref[...]

        _chunk_rows_loop(c_rows_ref[c], cap, row_tile, tile, o_ref)


def _expert_down(act, w_down, b_down, plan, cap, row_tile, n_chunks):
    c_exp, c_rows, n_used = plan
    n_exp, f, d = w_down.shape
    tn = _pick(d, 512, LANES)
    nj = d // tn
    cc, jj = _chunk_maps(nj)
    return pl.pallas_call(
        functools.partial(_expert_down_body, row_tile),
        grid_spec=pltpu.PrefetchScalarGridSpec(
            num_scalar_prefetch=3,
            grid=(n_chunks, nj),
            in_specs=[
                pl.BlockSpec((cap, f), lambda c, j, ce, cr, nu: (cc(c, nu), 0)),
                pl.BlockSpec((None, f, tn), lambda c, j, ce, cr, nu: (ce[cc(c, nu)], 0, jj(c, j, nu))),
                pl.BlockSpec((None, 1, tn), lambda c, j, ce, cr, nu: (ce[cc(c, nu)], 0, jj(c, j, nu))),
            ],
            out_specs=pl.BlockSpec((cap, tn), lambda c, j, ce, cr, nu: (cc(c, nu), jj(c, j, nu)))),
        out_shape=jax.ShapeDtypeStruct((n_chunks * cap, d), _F32),
        compiler_params=_params(("arbitrary", "arbitrary")),
        name="expert_down",
    )(c_exp, c_rows, n_used, act, w_down, b_down.reshape(n_exp, 1, d))


def _combine_body(dest_ref, gate_ref, y_ref, o_ref, buf_ref, sem):
    tc = o_ref.shape[0]

    def row_copy(t, k):
        return pltpu.make_async_copy(y_ref.at[pl.ds(dest_ref[0, t * TOP_K + k], 1)],
                                     buf_ref.at[k, pl.ds(t, 1)], sem)

    def issue(t, _):
        for k in range(TOP_K):
            row_copy(t, k).start()
        return 0

    lax.fori_loop(0, tc, issue, 0, unroll=DMA_ISSUE_UNROLL // TOP_K)
    for k in range(TOP_K):
        pltpu.make_async_copy(y_ref.at[pl.ds(0, tc)], buf_ref.at[k], sem).wait()
    gate = gate_ref[...]
    acc = buf_ref[0] * gate[:, 0:1]
    for k in range(1, TOP_K):
        acc = acc + buf_ref[k] * gate[:, k:k + 1]
    o_ref[...] = acc


def _combine(y, dest, gate):
    m = gate.shape[0]
    d = y.shape[1]
    tc = _pick(m, 128, SUBLANES)
    n_tiles = m // tc
    return pl.pallas_call(
        _combine_body,
        grid=(n_tiles,),
        in_specs=[pl.BlockSpec((None, 1, tc * TOP_K), lambda i: (i, 0, 0), memory_space=pltpu.SMEM),
                  pl.BlockSpec((tc, LANES), lambda i: (i, 0)),
                  pl.BlockSpec(memory_space=pl.ANY)],
        out_specs=pl.BlockSpec((tc, d), lambda i: (i, 0)),
        out_shape=jax.ShapeDtypeStruct((m, d), _F32),
        scratch_shapes=[pltpu.VMEM((TOP_K, tc, d), _F32), pltpu.SemaphoreType.DMA(())],
        compiler_params=_params(("arbitrary",)),
        name="moe_combine",
    )(dest.reshape(n_tiles, 1, tc * TOP_K), gate, y)


def _moe(h2, h2p, w_router, b_router, w_gate_up, b_gate_up, w_down, b_down):
    m = h2.shape[0]
    n_exp = w_router.shape[1]
    idx, gate, rank, counts = _router(h2, w_router, b_router)
    cap, row_tile, n_chunks = _moe_layout(m, n_exp)
    dest, plan, gather = _moe_plan(idx[:, :TOP_K], rank[:, :TOP_K], counts[0], cap, row_tile, n_chunks)
    xs = _dispatch(h2p, gather, cap, row_tile, n_chunks)
    act = _expert_up(xs, w_gate_up, b_gate_up, plan, cap, row_tile, n_chunks)
    y = _expert_down(act, w_down, b_down, plan, cap, row_tile, n_chunks)
    return _combine(y, dest, gate)


def kernel(x_prompt, x_sample, c_prompt, c_sample, cache_k_w128, cache_v_w128, cache_k_w512, cache_v_w512,
           cache_k_w2048, cache_v_w2048, state_conv, w_ada, b_ada, g_pre_mix, g_post_mix, g_pre_ffn, g_post_ffn,
           w_in, w_proj_a, w_dw, b_dw, g_conv_ln, b_conv_ln, w_pw2, b_pw2, w_o, w_router, b_router,
           w_gate_up, b_gate_up, w_down, b_down):
    B, S, D = x_prompt.shape
    DB, T, _ = x_sample.shape
    depth = w_ada.shape[0]
    C = w_dw.shape[2]
    H = HEADS_PER_GROUP
    bs, dbt = B * S, DB * T
    assert S % dbt == 0 and dbt % BF16_ROWS == 0 and S % ATTN_BLOCK == 0 and S >= CONV_HALO
    dims = (B, S, DB, T, D)
    k_caches = (cache_k_w128, cache_k_w512, cache_k_w2048)
    v_caches = (cache_v_w128, cache_v_w512, cache_v_w2048)
    glu_off = 3 * ATTN_WIDTH
    gate_off = glu_off + 2 * C
    tables = _rope_tables(B, S, DB, T, PAST_LEN)
    n_seq = B + DB
    c_all = jnp.concatenate([c_prompt, c_sample], axis=0)
    c_pad = jnp.pad(c_all, ((0, -n_seq % SUBLANES), (0, 0)))
    vec = lambda v: v.reshape(1, -1)

    x = (x_prompt.reshape(bs, D), x_sample.reshape(dbt, D))
    prompt_states, sample_states = [], []
    for l in range(depth):
        mod = _adaln(c_pad, w_ada[l], b_ada[l])
        (h,) = _tokenwise(_pre_mix_body, [x], [0, 1], [vec(g_pre_mix[l])], [(_BF, D, False)], mod, dims)
        proj = _in_proj(h, w_in[l], tables)

        attn_p = _attn_prompt(proj, B, S)
        qkv_s = proj[bs:, :3 * ATTN_WIDTH].reshape(DB, T, 9 * H, HEAD_DIM)
        attn_s = _attn_sample(qkv_s, [c[l] for c in k_caches], [c[l] for c in v_caches], DB, T)
        attn = jnp.concatenate([attn_p, attn_s.reshape(dbt, ATTN_OUT).astype(_BF)], axis=0)

        act_p, tail_p = _conv_prompt(proj, w_dw[l], b_dw[l], g_conv_ln[l], b_conv_ln[l], B, S, glu_off)
        glu_s = proj[bs:, glu_off:gate_off].reshape(DB, T, 2 * C)
        act_s, u_s = _conv_sample(glu_s, state_conv[l], w_dw[l], b_dw[l], g_conv_ln[l], b_conv_ln[l])
        act = jnp.concatenate([act_p, act_s.reshape(dbt, C).astype(_BF)], axis=0)

        z = _branch_merge(attn, w_proj_a[l], act, w_pw2[l], b_pw2[l], proj, gate_off)
        y = _matmul(z, w_o[l], _F32, "out_proj")
        x1, h2, h2p = _tokenwise(_post_mix_body, [x, y], [2, 3, 4], [vec(g_post_mix[l]), vec(g_pre_ffn[l])],
                                 [(_F32, D, False), (_F32, D, False), (jnp.uint32, D // 2, False)], mod, dims)
        f = _moe(h2, h2p, w_router[l], b_router[l], w_gate_up[l], b_gate_up[l], w_down[l], b_down[l])
        (x,) = _tokenwise(_post_ffn_body, [x1, f], [5], [vec(g_post_ffn[l])], [(_F32, D, True)], mod, dims)

        st_p, st_s = [], []
        for g, (win, _) in enumerate(DILATED_GROUPS):
            for part, caches in ((1, k_caches), (2, v_caches)):
                lo = (part * N_GROUPS + g) * ATTN_OUT
                keep = min(win, S)
                cols = lax.slice(proj, (0, lo), (bs, lo + ATTN_OUT)).reshape(B, S, H, HEAD_DIM)
                st_p.append(cols[:, S - keep:])
                new = proj[bs:, lo:lo + ATTN_OUT].reshape(DB, T, H, HEAD_DIM)
                n_past = caches[g].shape[2]
                keep_s = min(win, n_past + T)
                st_s.append(jnp.concatenate([caches[g][l], new], axis=1)[:, n_past + T - keep_s:])
        st_p.append(tail_p[:, CONV_HALO - (CONV_WIDTH - 1):])
        st_s.append(jnp.concatenate([state_conv[l], u_s], axis=1)[:, T:])
        prompt_states.append(st_p)
        sample_states.append(st_s)

    outs_p = [jnp.stack(s, axis=0) for s in zip(*prompt_states)]
    outs_s = [jnp.stack(s, axis=0) for s in zip(*sample_states)]
    return (x[0].reshape(B, S, D), x[1].reshape(DB, T, D), *outs_p, *outs_s)
```
